```python
import math
import jax, jax.numpy as jnp
from jax import lax
import numpy as np

D_MODEL = 4096
BATCH = 1
SEQ = 8192
DEPTH = 1

D_MIX = 2 * D_MODEL
GLA_WIDTH = D_MIX // 2
SSD_WIDTH = D_MIX - GLA_WIDTH

GLA_HEADS = 8
GLA_DV = GLA_WIDTH // GLA_HEADS
GLA_DK = GLA_DV // 2
GLA_QK = GLA_HEADS * GLA_DK
GLA_GATE_RANK = 16
GLA_GATE_NORMALIZER = 16.0
GLA_CHUNK = 64

SSD_HEAD_DIM = 64
SSD_HEADS = SSD_WIDTH // SSD_HEAD_DIM
SSD_STATE = 128
SSD_GROUPS = 8
SSD_HEADS_PER_GROUP = SSD_HEADS // SSD_GROUPS
SSD_CONV = 4
SSD_CHUNK = 128
SSD_XBC = SSD_WIDTH + 2 * SSD_GROUPS * SSD_STATE
SSD_NORM_GROUP = SSD_WIDTH // SSD_GROUPS
DT_MIN = 0.001
DT_MAX = 0.1

D_IN_PROJ = 2 * GLA_QK + 2 * GLA_WIDTH + GLA_GATE_RANK + SSD_WIDTH + SSD_XBC + SSD_HEADS

D_FF = 4 * D_MODEL
RMS_EPS = 1e-5

kernel_name = "hymba_gla_ssd_hybrid_layer"


def rms_norm(x, w, eps=RMS_EPS):
    xf = x.astype(jnp.float32)
    y = xf * lax.rsqrt(jnp.mean(xf * xf, axis=-1, keepdims=True) + eps)
    return (y * w.astype(jnp.float32)).astype(x.dtype)


def causal_depthwise_conv(x, w, b):
    K, C = w.shape
    y = lax.conv_general_dilated(
        x, w[:, None, :].astype(x.dtype), window_strides=(1,), padding=[(K - 1, 0)],
        dimension_numbers=('NWC', 'WIO', 'NWC'), feature_group_count=C)
    return y + b.astype(x.dtype)


def gla_mixer(q, k, v, g_out, gate_lr, gate_w2, gate_b, norm_w):
    Bsz, S, _ = q.shape
    H, DK, DV, C = GLA_HEADS, GLA_DK, GLA_DV, GLA_CHUNK
    NC = S // C

    def heads(t, d):
        return t.reshape(Bsz, NC, C, H, d).transpose(0, 3, 1, 2, 4)

    log_a = jax.nn.log_sigmoid((gate_lr @ gate_w2 + gate_b).astype(jnp.float32)) / GLA_GATE_NORMALIZER
    b = jnp.cumsum(heads(log_a, DK), axis=3)
    qh = heads(q, DK) * (DK ** -0.5)
    kh = heads(k, DK)
    vh = heads(v, DV)
    q_dec = (qh * jnp.exp(b)).astype(q.dtype)
    k_dec = (kh * jnp.exp(-b)).astype(q.dtype)
    k_end = (kh * jnp.exp(b[:, :, :, -1:, :] - b)).astype(q.dtype)
    chunk_decay = jnp.exp(b[:, :, :, -1, :])

    causal = jnp.tril(jnp.ones((C, C), dtype=bool))
    scores = jnp.einsum('bhncd,bhnsd->bhncs', q_dec, k_dec)
    scores = jnp.where(causal, scores, jnp.zeros((), scores.dtype))
    o_intra = jnp.einsum('bhncs,bhnsv->bhncv', scores, vh)

    def step(state, inp):
        qd, ke, vc, dec = inp
        o = jnp.einsum('bhcd,bhdv->bhcv', qd, state)
        state = dec[..., None] * state + jnp.einsum('bhcd,bhcv->bhdv', ke, vc).astype(jnp.float32)
        return state, o

    state0 = jnp.zeros((Bsz, H, DK, DV), jnp.float32)
    mv = lambda t: jnp.moveaxis(t, 2, 0)
    _, o_inter = lax.scan(step, state0, (mv(q_dec), mv(k_end), mv(vh), mv(chunk_decay)))
    o = o_intra + jnp.moveaxis(o_inter, 0, 2)
    o = o.transpose(0, 2, 3, 1, 4).reshape(Bsz, S, H, DV)
    o = rms_norm(o, norm_w) * jax.nn.silu(g_out.reshape(Bsz, S, H, DV))
    return o.reshape(Bsz, S, H * DV)


def ssd_mixer(z, xbc, dt_raw, conv_w, conv_b, dt_bias, a_log, d_skip, norm_w):
    Bsz, S, _ = z.shape
    G, HG, P, N, L = SSD_GROUPS, SSD_HEADS_PER_GROUP, SSD_HEAD_DIM, SSD_STATE, SSD_CHUNK
    NC = S // L

    xbc = jax.nn.silu(causal_depthwise_conv(xbc, conv_w, conv_b))
    xs, Bm, Cm = jnp.split(xbc, [SSD_WIDTH, SSD_WIDTH + G * N], axis=-1)
    dt = jax.nn.softplus(dt_raw.astype(jnp.float32) + dt_bias.astype(jnp.float32))
    A = -jnp.exp(a_log.astype(jnp.float32)).reshape(G, HG)

    x = xs.reshape(Bsz, NC, L, G, HG, P)
    Bm = Bm.reshape(Bsz, NC, L, G, N)
    Cm = Cm.reshape(Bsz, NC, L, G, N)
    dt_c = dt.reshape(Bsz, NC, L, G, HG)
    a_cum = jnp.cumsum(dt_c * A, axis=2)
    xdt = x * dt_c[..., None]

    at = a_cum.transpose(0, 1, 3, 4, 2)
    seg = at[..., :, None] - at[..., None, :]
    causal = jnp.tril(jnp.ones((L, L), dtype=bool))
    decay = jnp.exp(jnp.where(causal, seg, -jnp.inf))
    cb = jnp.einsum('bcLgn,bcSgn->bcgLS', Cm, Bm)
    w = cb[:, :, :, None] * decay
    y_diag = jnp.einsum('bcghLS,bcSghp->bcLghp', w, xdt)

    xw = xdt * jnp.exp(a_cum[:, :, -1:] - a_cum)[..., None]
    states = jnp.einsum('bcLgn,bcLghp->bcghpn', Bm, xw)
    chunk_decay = jnp.exp(a_cum[:, :, -1])

    def step(h, inp):
        s, d = inp
        return d[..., None, None] * h + s, h

    h0 = jnp.zeros((Bsz, G, HG, P, N), jnp.float32)
    _, prev = lax.scan(step, h0, (jnp.moveaxis(states, 1, 0).astype(jnp.float32),
                                   jnp.moveaxis(chunk_decay, 1, 0)))
    prev = jnp.moveaxis(prev, 0, 1)
    y_off = jnp.einsum('bcLgn,bcghpn->bcLghp', Cm, prev) * jnp.exp(a_cum)[..., None]

    y = y_diag + y_off + x * d_skip.reshape(G, HG)[:, :, None]
    y = y.reshape(Bsz, S, SSD_WIDTH)
    yg = (y * jax.nn.silu(z)).reshape(Bsz, S, SSD_WIDTH // SSD_NORM_GROUP, SSD_NORM_GROUP)
    yg = rms_norm(yg, norm_w.reshape(SSD_WIDTH // SSD_NORM_GROUP, SSD_NORM_GROUP))
    return yg.reshape(Bsz, S, SSD_WIDTH)


def setup_inputs(seed: int = 0) -> dict:
    key = jax.random.key(seed)
    ks = jax.random.split(key, 20)
    f32 = jnp.float32

    def normal(k, shape, scale):
        return jax.random.normal(k, shape, f32) * scale

    def gain(k, shape):
        return 1.0 + 0.02 * jax.random.normal(k, shape, f32)

    x = normal(ks[0], (BATCH, SEQ, D_MODEL), 1.0)
    norm_mix_w = gain(ks[1], (DEPTH, D_MODEL))
    w_in = normal(ks[2], (DEPTH, D_MODEL, D_IN_PROJ), D_MODEL ** -0.5)
    gla_gate_w2 = normal(ks[3], (DEPTH, GLA_GATE_RANK, GLA_QK), GLA_GATE_RANK ** -0.5)
    gla_gate_b = normal(ks[4], (DEPTH, GLA_QK), 0.1)
    gla_norm_w = gain(ks[5], (DEPTH, GLA_DV))
    conv_w = normal(ks[6], (DEPTH, SSD_CONV, SSD_XBC), SSD_CONV ** -0.5)
    conv_b = normal(ks[7], (DEPTH, SSD_XBC), 0.02)
    dt = jnp.exp(jax.random.uniform(ks[8], (DEPTH, SSD_HEADS), f32, math.log(DT_MIN), math.log(DT_MAX)))
    dt_bias = dt + jnp.log(-jnp.expm1(-dt))
    a_log = jnp.log(jax.random.uniform(ks[9], (DEPTH, SSD_HEADS), f32, 1.0, 16.0))
    d_skip = 1.0 + 0.1 * jax.random.normal(ks[10], (DEPTH, SSD_HEADS), f32)
    ssd_norm_w = gain(ks[11], (DEPTH, SSD_WIDTH))
    w_out = normal(ks[12], (DEPTH, D_MIX, D_MODEL), D_MIX ** -0.5)
    norm_mlp_w = gain(ks[13], (DEPTH, D_MODEL))
    w_up = normal(ks[14], (DEPTH, D_MODEL, D_FF), D_MODEL ** -0.5)
    w_down = normal(ks[15], (DEPTH, D_FF, D_MODEL), D_FF ** -0.5)
    norm_f_w = gain(ks[16], (D_MODEL,))
    return {"x": x, "norm_mix_w": norm_mix_w, "w_in": w_in, "gla_gate_w2": gla_gate_w2,
            "gla_gate_b": gla_gate_b, "gla_norm_w": gla_norm_w, "conv_w": conv_w,
            "conv_b": conv_b, "dt_bias": dt_bias, "a_log": a_log, "d_skip": d_skip,
            "ssd_norm_w": ssd_norm_w, "w_out": w_out, "norm_mlp_w": norm_mlp_w,
            "w_up": w_up, "w_down": w_down, "norm_f_w": norm_f_w}


def reference(x, norm_mix_w, w_in, gla_gate_w2, gla_gate_b, gla_norm_w, conv_w, conv_b,
              dt_bias, a_log, d_skip, ssd_norm_w, w_out, norm_mlp_w, w_up, w_down, norm_f_w):
    sizes = [GLA_QK, GLA_QK, GLA_WIDTH, GLA_WIDTH, GLA_GATE_RANK, SSD_WIDTH, SSD_XBC, SSD_HEADS]
    splits = [int(s) for s in np.cumsum(sizes)[:-1]]
    h = x
    for l in range(DEPTH):
        u = rms_norm(h, norm_mix_w[l])
        proj = u @ w_in[l]
        q, k, v, g_out, gate_lr, z, xbc, dt_raw = jnp.split(proj, splits, axis=-1)
        o_gla = gla_mixer(q, k, v, g_out, gate_lr, gla_gate_w2[l], gla_gate_b[l],
                          gla_norm_w[l]).astype(x.dtype)
        o_ssd = ssd_mixer(z, xbc, dt_raw, conv_w[l], conv_b[l], dt_bias[l], a_log[l],
                          d_skip[l], ssd_norm_w[l]).astype(x.dtype)
        h = h + jnp.concatenate([o_gla, o_ssd], axis=-1) @ w_out[l]
        u = rms_norm(h, norm_mlp_w[l])
        h = h + jnp.square(jax.nn.relu(u @ w_up[l])) @ w_down[l]
    return rms_norm(h, norm_f_w)
```

```python
import functools
import math

import jax
import jax.numpy as jnp
from jax import lax
from jax.experimental import pallas as pl
from jax.experimental.pallas import tpu as pltpu

F32 = jnp.float32
BF16 = jnp.bfloat16

D_MODEL = 4096
GLA_HEADS = 8
GLA_DK = 256
GLA_DV = 512
GLA_QK = GLA_HEADS * GLA_DK
GLA_WIDTH = GLA_HEADS * GLA_DV
GLA_GATE_RANK = 16
GLA_GATE_NORMALIZER = 16.0
GLA_CHUNK = 64
SSD_WIDTH = 4096
SSD_HEAD_DIM = 64
SSD_HEADS = SSD_WIDTH // SSD_HEAD_DIM
SSD_STATE = 128
SSD_GROUPS = 8
SSD_HEADS_PER_GROUP = SSD_HEADS // SSD_GROUPS
SSD_GROUP_WIDTH = SSD_HEADS_PER_GROUP * SSD_HEAD_DIM
SSD_CONV = 4
SSD_CHUNK = 128
SSD_XBC = SSD_WIDTH + 2 * SSD_GROUPS * SSD_STATE
RMS_EPS = 1e-5

LANES = 128
SUBLANES = 8
VMEM_BYTES_V7X = 64 * 1024 * 1024

OFF_Q = 0
OFF_K = OFF_Q + GLA_QK
OFF_V = OFF_K + GLA_QK
OFF_G = OFF_V + GLA_WIDTH
OFF_Z = OFF_G + GLA_WIDTH
OFF_X = OFF_Z + SSD_WIDTH
OFF_B = OFF_X + SSD_WIDTH
OFF_C = OFF_B + SSD_GROUPS * SSD_STATE
N_MAIN = OFF_C + SSD_GROUPS * SSD_STATE
SMALL_DT_OFF = LANES - SSD_HEADS


def _vmem_limit(nbytes):
    return int(min(nbytes + (8 << 20), VMEM_BYTES_V7X - (4 << 20)))


def _dot(a, b):
    return jnp.dot(a, b, preferred_element_type=F32)


def _dot_nt(a, b):
    return lax.dot_general(a, b, (((1,), (1,)), ((), ())), preferred_element_type=F32)


def _dot_tn(a, b):
    return lax.dot_general(a, b, (((0,), (0,)), ((), ())), preferred_element_type=F32)


def _split3(x):
    hi = x.astype(BF16)
    r1 = x - hi.astype(F32)
    mid = r1.astype(BF16)
    lo = (r1 - mid.astype(F32)).astype(BF16)
    return hi, mid, lo


def _select_dot(sel_bf16, x):
    hi, mid, lo = _split3(x)
    return _dot(sel_bf16, hi) + _dot(sel_bf16, mid) + _dot(sel_bf16, lo)


def _dot_select(x, sel_bf16):
    hi, mid, lo = _split3(x)
    return _dot(hi, sel_bf16) + _dot(mid, sel_bf16) + _dot(lo, sel_bf16)


def _softplus(x):
    return jnp.maximum(x, 0.0) + jnp.log1p(jnp.exp(-jnp.abs(x)))


def _log_sigmoid(x):
    return jnp.minimum(x, 0.0) - jnp.log1p(jnp.exp(-jnp.abs(x)))


def _silu(x):
    return x * jax.nn.sigmoid(x)


def _rmsnorm_kernel(x_ref, w_ref, o_ref):
    x = x_ref[...]
    ms = jnp.mean(x * x, axis=-1, keepdims=True)
    o_ref[...] = (x * lax.rsqrt(ms + RMS_EPS) * w_ref[...]).astype(o_ref.dtype)


def _rmsnorm(x, w, out_dtype, tm=256):
    m, d = x.shape
    nbytes = 2 * tm * d * (4 + jnp.dtype(out_dtype).itemsize) + 2 * d * 4
    return pl.pallas_call(
        _rmsnorm_kernel,
        grid=(m // tm,),
        in_specs=[pl.BlockSpec((tm, d), lambda i: (i, 0)),
                  pl.BlockSpec((1, d), lambda i: (0, 0))],
        out_specs=pl.BlockSpec((tm, d), lambda i: (i, 0)),
        out_shape=jax.ShapeDtypeStruct((m, d), out_dtype),
        compiler_params=pltpu.CompilerParams(
            dimension_semantics=("arbitrary",), vmem_limit_bytes=_vmem_limit(nbytes)),
        name="rmsnorm",
    )(x, w.reshape(1, d))


def _matmul_kernel(*refs, nk, relu_sq, has_res):
    a_ref, b_ref = refs[0], refs[1]
    res_ref = refs[2] if has_res else None
    o_ref = refs[2 + has_res]

    def finish(acc):
        if relu_sq:
            acc = jnp.square(jnp.maximum(acc, 0.0))
        if has_res:
            acc = acc + res_ref[...]
        o_ref[...] = acc.astype(o_ref.dtype)

    part = _dot(a_ref[...], b_ref[...])
    if nk == 1:
        finish(part)
        return
    acc_ref = refs[3 + has_res]
    k = pl.program_id(2)

    @pl.when(k == 0)
    def _():
        acc_ref[...] = part

    @pl.when(jnp.logical_and(k > 0, k < nk - 1))
    def _():
        acc_ref[...] += part

    @pl.when(k == nk - 1)
    def _():
        finish(acc_ref[...] + part)


def _matmul(a, b, *, tm, tn, tk, out_dtype, residual=None, relu_sq=False, name):
    m, kdim = a.shape
    _, n = b.shape
    nk = kdim // tk
    has_res = residual is not None
    in_specs = [pl.BlockSpec((tm, tk), lambda j, i, k: (i, k)),
                pl.BlockSpec((tk, tn), lambda j, i, k: (k, j))]
    args = [a, b]
    if has_res:
        in_specs.append(pl.BlockSpec((tm, tn), lambda j, i, k: (i, j)))
        args.append(residual)
    out_bytes = jnp.dtype(out_dtype).itemsize
    nbytes = (2 * (tm * tk + tk * tn) * a.dtype.itemsize
              + 2 * tm * tn * (out_bytes + 4 * has_res)
              + tm * tn * 4 * (1 + (nk > 1)))
    return pl.pallas_call(
        functools.partial(_matmul_kernel, nk=nk, relu_sq=relu_sq, has_res=has_res),
        grid=(n // tn, m // tm, nk),
        in_specs=in_specs,
        out_specs=pl.BlockSpec((tm, tn), lambda j, i, k: (i, j)),
        out_shape=jax.ShapeDtypeStruct((m, n), out_dtype),
        scratch_shapes=[pltpu.VMEM((tm, tn), F32)] if nk > 1 else [],
        compiler_params=pltpu.CompilerParams(
            dimension_semantics=("arbitrary", "arbitrary", "arbitrary"),
            vmem_limit_bytes=_vmem_limit(nbytes)),
        name=name,
    )(*args)


GLA_BLOCK = 512


def _gla_kernel(q_ref, k_ref, v_ref, g_ref, sm_ref, w2_ref, gb_ref, nw_ref,
                o_ref, state_ref):
    c = GLA_CHUNK

    @pl.when(pl.program_id(1) == 0)
    def _():
        state_ref[...] = jnp.zeros_like(state_ref)

    row = lax.broadcasted_iota(jnp.int32, (c, c), 0)
    col = lax.broadcasted_iota(jnp.int32, (c, c), 1)
    causal = row >= col
    tril = jnp.where(causal, 1.0, 0.0).astype(BF16)

    pre = _dot(sm_ref[...].astype(BF16), w2_ref[...]) + gb_ref[...]
    log_a = _log_sigmoid(pre) * (1.0 / GLA_GATE_NORMALIZER)

    for n in range(GLA_BLOCK // c):
        rows = slice(n * c, (n + 1) * c)
        b = _select_dot(tril, log_a[rows, :])
        b_last = b[c - 1:c, :]
        qn = q_ref[rows, :] * (GLA_DK ** -0.5)
        kn = k_ref[rows, :]
        vn = v_ref[rows, :].astype(BF16)
        q_dec = (qn * jnp.exp(b)).astype(BF16)
        k_dec = (kn * jnp.exp(-b)).astype(BF16)
        k_end = (kn * jnp.exp(b_last - b)).astype(BF16)
        scores = jnp.where(causal, _dot_nt(q_dec, k_dec), 0.0)
        state = state_ref[...]
        o = _dot(scores.astype(BF16), vn) + _dot_nt(q_dec, state.astype(BF16))
        state_ref[...] = state * jnp.exp(b_last) + _dot_tn(vn, k_end)
        ms = jnp.mean(o * o, axis=-1, keepdims=True)
        o = o * lax.rsqrt(ms + RMS_EPS) * nw_ref[...]
        o_ref[rows, :] = (o * _silu(g_ref[rows, :])).astype(o_ref.dtype)


def _gla_mixer(proj, small, w2_pad, gate_b, norm_w):
    s = proj.shape[0]
    t = GLA_BLOCK
    dkb, dvb = GLA_DK, GLA_DV
    nbytes = 2 * t * (2 * dkb + 2 * dvb + LANES) * 4 + 2 * t * dvb * 2 + dvb * dkb * 4
    return pl.pallas_call(
        _gla_kernel,
        grid=(GLA_HEADS, s // t),
        in_specs=[
            pl.BlockSpec((t, dkb), lambda h, i: (i, OFF_Q // dkb + h)),
            pl.BlockSpec((t, dkb), lambda h, i: (i, OFF_K // dkb + h)),
            pl.BlockSpec((t, dvb), lambda h, i: (i, OFF_V // dvb + h)),
            pl.BlockSpec((t, dvb), lambda h, i: (i, OFF_G // dvb + h)),
            pl.BlockSpec((t, LANES), lambda h, i: (i, 0)),
            pl.BlockSpec((LANES, dkb), lambda h, i: (0, h)),
            pl.BlockSpec((1, dkb), lambda h, i: (0, h)),
            pl.BlockSpec((1, dvb), lambda h, i: (0, 0)),
        ],
        out_specs=pl.BlockSpec((t, dvb), lambda h, i: (i, h)),
        out_shape=jax.ShapeDtypeStruct((s, GLA_WIDTH), BF16),
        scratch_shapes=[pltpu.VMEM((dvb, dkb), F32)],
        compiler_params=pltpu.CompilerParams(
            dimension_semantics=("arbitrary", "arbitrary"),
            vmem_limit_bytes=_vmem_limit(nbytes)),
        name="gla_mixer",
    )(proj, proj, proj, proj, small, w2_pad, gate_b.reshape(1, -1), norm_w.reshape(1, -1))


CONV_PAD = SUBLANES
HEAD_DIM_SHIFT = SSD_HEAD_DIM.bit_length() - 1
CHUNK_SHIFT = SSD_CHUNK.bit_length() - 1


def _ssd_kernel(xr_ref, br_ref, cr_ref, z_ref, sm_ref,
                cwx_ref, cwb_ref, cwc_ref, cbx_ref, cbb_ref, cbc_ref,
                dtb_ref, alog_ref, dsk_ref, nw_ref,
                o_ref, xpad_ref, bpad_ref, cpad_ref, cumt_ref, state_ref):
    L = SSD_CHUNK
    gw = SSD_GROUP_WIDTH
    g = pl.program_id(0)

    @pl.when(pl.program_id(1) == 0)
    def _():
        xpad_ref[0:CONV_PAD, :] = jnp.zeros((CONV_PAD, gw), F32)
        bpad_ref[0:CONV_PAD, :] = jnp.zeros((CONV_PAD, SSD_STATE), F32)
        cpad_ref[0:CONV_PAD, :] = jnp.zeros((CONV_PAD, SSD_STATE), F32)
        state_ref[...] = jnp.zeros_like(state_ref)

    def conv_silu(raw_ref, pad_ref, w_ref, b_ref):
        pad_ref[CONV_PAD:CONV_PAD + L, :] = raw_ref[...]
        acc = b_ref[...]
        for k in range(SSD_CONV):
            acc = acc + w_ref[k:k + 1, :] * pad_ref[pl.ds(CONV_PAD - (SSD_CONV - 1) + k, L), :]
        pad_ref[0:CONV_PAD, :] = pad_ref[L:L + CONV_PAD, :]
        return _silu(acc)

    xs = conv_silu(xr_ref, xpad_ref, cwx_ref, cbx_ref)
    bm = conv_silu(br_ref, bpad_ref, cwb_ref, cbb_ref).astype(BF16)
    cm = conv_silu(cr_ref, cpad_ref, cwc_ref, cbc_ref).astype(BF16)

    dt = _softplus(sm_ref[...] + dtb_ref[...])
    a = dt * (-jnp.exp(alog_ref[...]))
    row = lax.broadcasted_iota(jnp.int32, (L, L), 0)
    col = lax.broadcasted_iota(jnp.int32, (L, L), 1)
    causal = row >= col
    tril = jnp.where(causal, 1.0, 0.0).astype(BF16)
    a_cum = _select_dot(tril, a)
    cumt_ref[...] = a_cum.T

    base = SMALL_DT_OFF + g * SSD_HEADS_PER_GROUP
    sel_r = lax.broadcasted_iota(jnp.int32, (LANES, gw), 0)
    sel_c = lax.broadcasted_iota(jnp.int32, (LANES, gw), 1)
    sel = jnp.where(sel_r == base + (sel_c >> HEAD_DIM_SHIFT), 1.0, 0.0).astype(BF16)
    hw = SSD_HEADS_PER_GROUP * L
    selw_r = lax.broadcasted_iota(jnp.int32, (LANES, hw), 0)
    selw_c = lax.broadcasted_iota(jnp.int32, (LANES, hw), 1)
    selw = jnp.where(selw_r == base + (selw_c >> CHUNK_SHIFT), 1.0, 0.0).astype(BF16)

    dt_x = _dot_select(dt, sel)
    cum_x = _dot_select(a_cum, sel)
    cum_w = _dot_select(a_cum, selw)
    last_x = cum_x[L - 1:L, :]

    xdt = xs * dt_x
    xdt_bf = xdt.astype(BF16)
    cb = _dot_nt(cm, bm)

    lane = lax.broadcasted_iota(jnp.int32, (L, 2 * SSD_HEAD_DIM), 1)
    first_half = lane < SSD_HEAD_DIM
    y_parts = []
    for p in range(SSD_HEADS_PER_GROUP // 2):
        xpair = xdt_bf[:, p * 2 * SSD_HEAD_DIM:(p + 1) * 2 * SSD_HEAD_DIM]
        ys = []
        for j in (2 * p, 2 * p + 1):
            colb = cum_w[:, j * L:(j + 1) * L]
            rowb = cumt_ref[pl.ds(base + j, 1), :]
            decay = jnp.where(causal, jnp.exp(colb - rowb), 0.0)
            ys.append(_dot((cb * decay).astype(BF16), xpair))
        y_parts.append(jnp.where(first_half, ys[0], ys[1]))
    y_diag = jnp.concatenate(y_parts, axis=1)

    prev = state_ref[...]
    y_off = _dot(cm, prev.astype(BF16)) * jnp.exp(cum_x)
    xw = (xdt * jnp.exp(last_x - cum_x)).astype(BF16)
    state_ref[...] = prev * jnp.exp(last_x) + _dot_tn(bm, xw)

    y = y_diag + y_off + xs * dsk_ref[...]
    yg = y * _silu(z_ref[...])
    ms = jnp.mean(yg * yg, axis=-1, keepdims=True)
    o_ref[...] = (yg * lax.rsqrt(ms + RMS_EPS) * nw_ref[...]).astype(o_ref.dtype)


def _ssd_mixer(proj, small, conv_w, conv_b, dtb_pad, alog_pad, dskip_x, norm_w):
    s = proj.shape[0]
    L, gw, n = SSD_CHUNK, SSD_GROUP_WIDTH, SSD_STATE
    xb, bb, cbk = SSD_WIDTH // gw, SSD_WIDTH // n, (SSD_WIDTH + SSD_GROUPS * n) // n
    nbytes = (2 * L * (2 * gw + 2 * n + LANES) * 4 + 2 * L * gw * 2
              + (L + CONV_PAD) * (gw + 2 * n) * 4 + LANES * L * 4 + n * gw * 4)
    return pl.pallas_call(
        _ssd_kernel,
        grid=(SSD_GROUPS, s // L),
        in_specs=[
            pl.BlockSpec((L, gw), lambda g, c: (c, OFF_X // gw + g)),
            pl.BlockSpec((L, n), lambda g, c: (c, OFF_B // n + g)),
            pl.BlockSpec((L, n), lambda g, c: (c, OFF_C // n + g)),
            pl.BlockSpec((L, gw), lambda g, c: (c, OFF_Z // gw + g)),
            pl.BlockSpec((L, LANES), lambda g, c: (c, 0)),
            pl.BlockSpec((SSD_CONV, gw), lambda g, c: (0, g)),
            pl.BlockSpec((SSD_CONV, n), lambda g, c: (0, bb + g)),
            pl.BlockSpec((SSD_CONV, n), lambda g, c: (0, cbk + g)),
            pl.BlockSpec((1, gw), lambda g, c: (0, g)),
            pl.BlockSpec((1, n), lambda g, c: (0, bb + g)),
            pl.BlockSpec((1, n), lambda g, c: (0, cbk + g)),
            pl.BlockSpec((1, LANES), lambda g, c: (0, 0)),
            pl.BlockSpec((1, LANES), lambda g, c: (0, 0)),
            pl.BlockSpec((1, gw), lambda g, c: (0, g)),
            pl.BlockSpec((1, gw), lambda g, c: (0, g)),
        ],
        out_specs=pl.BlockSpec((L, gw), lambda g, c: (c, g)),
        out_shape=jax.ShapeDtypeStruct((s, SSD_WIDTH), BF16),
        scratch_shapes=[
            pltpu.VMEM((L + CONV_PAD, gw), F32),
            pltpu.VMEM((L + CONV_PAD, n), F32),
            pltpu.VMEM((L + CONV_PAD, n), F32),
            pltpu.VMEM((LANES, L), F32),
            pltpu.VMEM((n, gw), F32),
        ],
        compiler_params=pltpu.CompilerParams(
            dimension_semantics=("arbitrary", "arbitrary"),
            vmem_limit_bytes=_vmem_limit(nbytes)),
        name="ssd_mixer",
    )(proj, proj, proj, proj, small,
      conv_w, conv_w, conv_w, conv_b, conv_b, conv_b,
      dtb_pad, alog_pad, dskip_x, norm_w.reshape(1, -1))


def kernel(x, norm_mix_w, w_in, gla_gate_w2, gla_gate_b, gla_norm_w, conv_w, conv_b,
           dt_bias, a_log, d_skip, ssd_norm_w, w_out, norm_mlp_w, w_up, w_down, norm_f_w):
    bsz, seq, d = x.shape
    h = x.reshape(bsz * seq, d)
    m = h.shape[0]
    depth = w_in.shape[0]
    gate_lo = OFF_G + GLA_WIDTH
    z_lo = gate_lo + GLA_GATE_RANK
    dt_lo = z_lo + SSD_WIDTH + SSD_XBC
    pad_heads = lambda v: jnp.pad(v, (SMALL_DT_OFF, 0)).reshape(1, LANES)

    for l in range(depth):
        w_main = jnp.concatenate([w_in[l][:, :gate_lo], w_in[l][:, z_lo:dt_lo]], axis=1).astype(BF16)
        w_small = jnp.concatenate(
            [w_in[l][:, gate_lo:z_lo],
             jnp.zeros((d, SMALL_DT_OFF - GLA_GATE_RANK), w_in.dtype),
             w_in[l][:, dt_lo:]], axis=1).astype(BF16)
        w2_pad = jnp.pad(gla_gate_w2[l], ((0, LANES - GLA_GATE_RANK), (0, 0))).astype(BF16)
        dskip_x = jnp.repeat(d_skip[l], SSD_HEAD_DIM).reshape(1, SSD_WIDTH)

        u = _rmsnorm(h, norm_mix_w[l], BF16)
        proj = _matmul(u, w_main, tm=1024, tn=1024, tk=d, out_dtype=F32, name="in_proj")
        small = _matmul(u, w_small, tm=1024, tn=LANES, tk=d, out_dtype=F32, name="in_proj_small")

        o_gla = _gla_mixer(proj, small, w2_pad, gla_gate_b[l], gla_norm_w[l])
        o_ssd = _ssd_mixer(proj, small, conv_w[l], conv_b[l].reshape(1, -1),
                           pad_heads(dt_bias[l]), pad_heads(a_log[l]), dskip_x, ssd_norm_w[l])
        mix = jnp.concatenate([o_gla, o_ssd], axis=1)

        h = _matmul(mix, w_out[l].astype(BF16), tm=1024, tn=1024, tk=2048,
                    out_dtype=F32, residual=h, name="out_proj")
        u = _rmsnorm(h, norm_mlp_w[l], BF16)
        hid = _matmul(u, w_up[l].astype(BF16), tm=1024, tn=1024, tk=d,
                      out_dtype=BF16, relu_sq=True, name="mlp_up")
        h = _matmul(hid, w_down[l].astype(BF16), tm=1024, tn=1024, tk=2048,
                    out_dtype=F32, residual=h, name="mlp_down")

    out = _rmsnorm(h, norm_f_w, x.dtype)
    return out.reshape(bsz, seq, d)
```

```python
import functools
import math

import jax
import jax.numpy as jnp
from jax import lax
from jax.experimental import pallas as pl
from jax.experimental.pallas import tpu as pltpu

F32 = jnp.float32
BF16 = jnp.bfloat16

D_MODEL = 4096
GLA_HEADS = 8
GLA_DK = 256
GLA_DV = 512
GLA_QK = GLA_HEADS * GLA_DK
GLA_WIDTH = GLA_HEADS * GLA_DV
GLA_GATE_RANK = 16
GLA_GATE_NORMALIZER = 16.0
GLA_CHUNK = 64
SSD_WIDTH = 4096
SSD_HEAD_DIM = 64
SSD_HEADS = SSD_WIDTH // SSD_HEAD_DIM
SSD_STATE = 128
SSD_GROUPS = 8
SSD_HEADS_PER_GROUP = SSD_HEADS // SSD_GROUPS
SSD_GROUP_WIDTH = SSD_HEADS_PER_GROUP * SSD_HEAD_DIM
SSD_CONV = 4
SSD_CHUNK = 128
SSD_XBC = SSD_WIDTH + 2 * SSD_GROUPS * SSD_STATE
RMS_EPS = 1e-5

LANES = 128
SUBLANES = 8
VMEM_BYTES_V7X = 64 * 1024 * 1024

OFF_Q = 0
OFF_K = OFF_Q + GLA_QK
OFF_V = OFF_K + GLA_QK
OFF_G = OFF_V + GLA_WIDTH
OFF_Z = OFF_G + GLA_WIDTH
OFF_X = OFF_Z + SSD_WIDTH
OFF_B = OFF_X + SSD_WIDTH
OFF_C = OFF_B + SSD_GROUPS * SSD_STATE
N_MAIN = OFF_C + SSD_GROUPS * SSD_STATE
SMALL_DT_OFF = LANES - SSD_HEADS


def _vmem_limit(nbytes):
    return int(min(nbytes + (8 << 20), VMEM_BYTES_V7X - (4 << 20)))


def _dot(a, b):
    return jnp.dot(a, b, preferred_element_type=F32)


def _dot_nt(a, b):
    return lax.dot_general(a, b, (((1,), (1,)), ((), ())), preferred_element_type=F32)


def _dot_tn(a, b):
    return lax.dot_general(a, b, (((0,), (0,)), ((), ())), preferred_element_type=F32)


def _split3(x):
    hi = x.astype(BF16)
    r1 = x - hi.astype(F32)
    mid = r1.astype(BF16)
    lo = (r1 - mid.astype(F32)).astype(BF16)
    return hi, mid, lo


def _select_dot(sel_bf16, x):
    hi, mid, lo = _split3(x)
    return _dot(sel_bf16, hi) + _dot(sel_bf16, mid) + _dot(sel_bf16, lo)


def _dot_select(x, sel_bf16):
    hi, mid, lo = _split3(x)
    return _dot(hi, sel_bf16) + _dot(mid, sel_bf16) + _dot(lo, sel_bf16)


def _softplus(x):
    return jnp.maximum(x, 0.0) + jnp.log1p(jnp.exp(-jnp.abs(x)))


def _log_sigmoid(x):
    return jnp.minimum(x, 0.0) - jnp.log1p(jnp.exp(-jnp.abs(x)))


def _silu(x):
    return x * jax.nn.sigmoid(x)


def _rmsnorm_kernel(x_ref, w_ref, o_ref):
    x = x_ref[...]
    ms = jnp.mean(x * x, axis=-1, keepdims=True)
    o_ref[...] = (x * lax.rsqrt(ms + RMS_EPS) * w_ref[...]).astype(o_ref.dtype)


def _rmsnorm(x, w, out_dtype, tm=256):
    m, d = x.shape
    nbytes = 2 * tm * d * (4 + jnp.dtype(out_dtype).itemsize) + 2 * d * 4
    return pl.pallas_call(
        _rmsnorm_kernel,
        grid=(m // tm,),
        in_specs=[pl.BlockSpec((tm, d), lambda i: (i, 0)),
                  pl.BlockSpec((1, d), lambda i: (0, 0))],
        out_specs=pl.BlockSpec((tm, d), lambda i: (i, 0)),
        out_shape=jax.ShapeDtypeStruct((m, d), out_dtype),
        compiler_params=pltpu.CompilerParams(
            dimension_semantics=("arbitrary",), vmem_limit_bytes=_vmem_limit(nbytes)),
        name="rmsnorm",
    )(x, w.reshape(1, d))


CAST_ROWS = 256
PROJ_TN = 512


def _proj_kernel(*refs, shift, shift_from, relu_sq):
    if shift:
        a_ref, w_ref, wx_ref, o_ref, wbf_ref = refs
    else:
        a_ref, w_ref, o_ref, wbf_ref = refs
    j = pl.program_id(0)
    kdim = w_ref.shape[0]

    def convert(shifted):
        def body(r, carry):
            rows = pl.ds(pl.multiple_of(r * CAST_ROWS, CAST_ROWS), CAST_ROWS)
            if shifted:
                w = jnp.concatenate([w_ref[rows, shift:], wx_ref[rows, :shift]], axis=1)
            else:
                w = w_ref[rows, :]
            wbf_ref[rows, :] = w.astype(BF16)
            return carry
        lax.fori_loop(0, kdim // CAST_ROWS, body, 0)

    @pl.when(pl.program_id(1) == 0)
    def _():
        if shift:
            @pl.when(j < shift_from)
            def _():
                convert(False)

            @pl.when(j >= shift_from)
            def _():
                convert(True)
        else:
            convert(False)

    acc = _dot(a_ref[...], wbf_ref[...])
    if relu_sq:
        acc = jnp.square(jnp.maximum(acc, 0.0))
    o_ref[...] = acc.astype(o_ref.dtype)


def _proj(a, w, *, n_out, tm, tn, out_dtype, shift=0, shift_from=0, relu_sq=False, name):
    m, kdim = a.shape
    in_specs = [pl.BlockSpec((tm, kdim), lambda j, i: (i, 0)),
                pl.BlockSpec((kdim, tn), lambda j, i: (0, j))]
    args = [a, w]
    if shift:
        per = tn // LANES
        in_specs.append(pl.BlockSpec(
            (kdim, LANES), lambda j, i: (0, jnp.where(j >= shift_from, (j + 1) * per, 0))))
        args.append(w)
    nbytes = (2 * tm * kdim * 2 + 2 * kdim * (tn + LANES * bool(shift)) * 4 + kdim * tn * 2
              + 2 * tm * tn * jnp.dtype(out_dtype).itemsize + tm * tn * 4)
    return pl.pallas_call(
        functools.partial(_proj_kernel, shift=shift, shift_from=shift_from, relu_sq=relu_sq),
        grid=(n_out // tn, m // tm),
        in_specs=in_specs,
        out_specs=pl.BlockSpec((tm, tn), lambda j, i: (i, j)),
        out_shape=jax.ShapeDtypeStruct((m, n_out), out_dtype),
        scratch_shapes=[pltpu.VMEM((kdim, tn), BF16)],
        compiler_params=pltpu.CompilerParams(
            dimension_semantics=("arbitrary", "arbitrary"),
            vmem_limit_bytes=_vmem_limit(nbytes)),
        name=name,
    )(*args)


def _matmul_small(a, b, *, tm, out_dtype, name):
    m, kdim = a.shape
    n = b.shape[1]

    def body(a_ref, b_ref, o_ref):
        o_ref[...] = _dot(a_ref[...], b_ref[...]).astype(o_ref.dtype)

    nbytes = 2 * (tm * kdim + kdim * n) * 2 + 3 * tm * n * 4
    return pl.pallas_call(
        body,
        grid=(m // tm,),
        in_specs=[pl.BlockSpec((tm, kdim), lambda i: (i, 0)),
                  pl.BlockSpec((kdim, n), lambda i: (0, 0))],
        out_specs=pl.BlockSpec((tm, n), lambda i: (i, 0)),
        out_shape=jax.ShapeDtypeStruct((m, n), out_dtype),
        compiler_params=pltpu.CompilerParams(
            dimension_semantics=("arbitrary",), vmem_limit_bytes=_vmem_limit(nbytes)),
        name=name,
    )(a, b)


def _matmul_res_kernel(*refs, nk, n_lhs):
    a_refs = refs[:n_lhs]
    b_ref, res_ref, o_ref, acc_ref = refs[n_lhs:]
    k = pl.program_id(2)

    @pl.when(k == 0)
    def _():
        acc_ref[...] = jnp.zeros_like(acc_ref)

    if n_lhs == 1:
        a = a_refs[0][...]
    else:
        a = jnp.where(k < nk // 2, a_refs[0][...], a_refs[1][...])
    acc_ref[...] += _dot(a, b_ref[...])

    @pl.when(k == nk - 1)
    def _():
        o_ref[...] = (acc_ref[...] + res_ref[...]).astype(o_ref.dtype)


def _matmul_res(lhs, b, residual, *, tm, tn, tk, name):
    n_lhs = len(lhs)
    m = lhs[0].shape[0]
    kdim, n = b.shape
    nk = kdim // tk
    half = nk // 2
    if n_lhs == 1:
        in_specs = [pl.BlockSpec((tm, tk), lambda j, i, k: (i, k))]
    else:
        in_specs = [pl.BlockSpec((tm, tk), lambda j, i, k: (i, jnp.minimum(k, half - 1))),
                    pl.BlockSpec((tm, tk), lambda j, i, k: (i, jnp.maximum(k - half, 0)))]
    in_specs += [pl.BlockSpec((tk, tn), lambda j, i, k: (k, j)),
                 pl.BlockSpec((tm, tn), lambda j, i, k: (i, j))]
    nbytes = 2 * (n_lhs * tm * tk + tk * tn) * 2 + 4 * tm * tn * 4 + 2 * tm * tn * 4
    return pl.pallas_call(
        functools.partial(_matmul_res_kernel, nk=nk, n_lhs=n_lhs),
        grid=(n // tn, m // tm, nk),
        in_specs=in_specs,
        out_specs=pl.BlockSpec((tm, tn), lambda j, i, k: (i, j)),
        out_shape=jax.ShapeDtypeStruct((m, n), residual.dtype),
        scratch_shapes=[pltpu.VMEM((tm, tn), F32)],
        compiler_params=pltpu.CompilerParams(
            dimension_semantics=("arbitrary", "arbitrary", "arbitrary"),
            vmem_limit_bytes=_vmem_limit(nbytes)),
        name=name,
    )(*lhs, b, residual)


GLA_BLOCK = 512


def _gla_kernel(q_ref, k_ref, v_ref, g_ref, sm_ref, w2_ref, gb_ref, nw_ref,
                o_ref, state_ref):
    c = GLA_CHUNK

    @pl.when(pl.program_id(1) == 0)
    def _():
        state_ref[...] = jnp.zeros_like(state_ref)

    row = lax.broadcasted_iota(jnp.int32, (c, c), 0)
    col = lax.broadcasted_iota(jnp.int32, (c, c), 1)
    causal = row >= col
    tril = jnp.where(causal, 1.0, 0.0).astype(BF16)

    pre = _dot(sm_ref[...].astype(BF16), w2_ref[...]) + gb_ref[...]
    log_a = _log_sigmoid(pre) * (1.0 / GLA_GATE_NORMALIZER)

    for n in range(GLA_BLOCK // c):
        rows = slice(n * c, (n + 1) * c)
        b = _select_dot(tril, log_a[rows, :])
        b_last = b[c - 1:c, :]
        qn = q_ref[rows, :] * (GLA_DK ** -0.5)
        kn = k_ref[rows, :]
        vn = v_ref[rows, :].astype(BF16)
        q_dec = (qn * jnp.exp(b)).astype(BF16)
        k_dec = (kn * jnp.exp(-b)).astype(BF16)
        k_end = (kn * jnp.exp(b_last - b)).astype(BF16)
        scores = jnp.where(causal, _dot_nt(q_dec, k_dec), 0.0)
        state = state_ref[...]
        o = _dot(scores.astype(BF16), vn) + _dot_nt(q_dec, state.astype(BF16))
        state_ref[...] = state * jnp.exp(b_last) + _dot_tn(vn, k_end)
        ms = jnp.mean(o * o, axis=-1, keepdims=True)
        o = o * lax.rsqrt(ms + RMS_EPS) * nw_ref[...]
        o_ref[rows, :] = (o * _silu(g_ref[rows, :])).astype(o_ref.dtype)


def _gla_mixer(proj, small, w2_pad, gate_b, norm_w):
    s = proj.shape[0]
    t = GLA_BLOCK
    dkb, dvb = GLA_DK, GLA_DV
    nbytes = 2 * t * (2 * dkb + 2 * dvb + LANES) * 4 + 2 * t * dvb * 2 + dvb * dkb * 4
    return pl.pallas_call(
        _gla_kernel,
        grid=(GLA_HEADS, s // t),
        in_specs=[
            pl.BlockSpec((t, dkb), lambda h, i: (i, OFF_Q // dkb + h)),
            pl.BlockSpec((t, dkb), lambda h, i: (i, OFF_K // dkb + h)),
            pl.BlockSpec((t, dvb), lambda h, i: (i, OFF_V // dvb + h)),
            pl.BlockSpec((t, dvb), lambda h, i: (i, OFF_G // dvb + h)),
            pl.BlockSpec((t, LANES), lambda h, i: (i, 0)),
            pl.BlockSpec((LANES, dkb), lambda h, i: (0, h)),
            pl.BlockSpec((1, dkb), lambda h, i: (0, h)),
            pl.BlockSpec((1, dvb), lambda h, i: (0, 0)),
        ],
        out_specs=pl.BlockSpec((t, dvb), lambda h, i: (i, h)),
        out_shape=jax.ShapeDtypeStruct((s, GLA_WIDTH), BF16),
        scratch_shapes=[pltpu.VMEM((dvb, dkb), F32)],
        compiler_params=pltpu.CompilerParams(
            dimension_semantics=("arbitrary", "arbitrary"),
            vmem_limit_bytes=_vmem_limit(nbytes)),
        name="gla_mixer",
    )(proj, proj, proj, proj, small, w2_pad, gate_b.reshape(1, -1), norm_w.reshape(1, -1))


CONV_PAD = SUBLANES
HEAD_DIM_SHIFT = SSD_HEAD_DIM.bit_length() - 1
CHUNK_SHIFT = SSD_CHUNK.bit_length() - 1


def _ssd_kernel(xr_ref, br_ref, cr_ref, z_ref, sm_ref,
                cwx_ref, cwb_ref, cwc_ref, cbx_ref, cbb_ref, cbc_ref,
                dtb_ref, alog_ref, dsk_ref, nw_ref,
                o_ref, xpad_ref, bpad_ref, cpad_ref, cumt_ref, state_ref):
    L = SSD_CHUNK
    gw = SSD_GROUP_WIDTH
    g = pl.program_id(0)

    @pl.when(pl.program_id(1) == 0)
    def _():
        xpad_ref[0:CONV_PAD, :] = jnp.zeros((CONV_PAD, gw), F32)
        bpad_ref[0:CONV_PAD, :] = jnp.zeros((CONV_PAD, SSD_STATE), F32)
        cpad_ref[0:CONV_PAD, :] = jnp.zeros((CONV_PAD, SSD_STATE), F32)
        state_ref[...] = jnp.zeros_like(state_ref)

    def conv_silu(raw_ref, pad_ref, w_ref, b_ref):
        pad_ref[CONV_PAD:CONV_PAD + L, :] = raw_ref[...]
        acc = b_ref[...]
        for k in range(SSD_CONV):
            acc = acc + w_ref[k:k + 1, :] * pad_ref[pl.ds(CONV_PAD - (SSD_CONV - 1) + k, L), :]
        pad_ref[0:CONV_PAD, :] = pad_ref[L:L + CONV_PAD, :]
        return _silu(acc)

    xs = conv_silu(xr_ref, xpad_ref, cwx_ref, cbx_ref)
    bm = conv_silu(br_ref, bpad_ref, cwb_ref, cbb_ref).astype(BF16)
    cm = conv_silu(cr_ref, cpad_ref, cwc_ref, cbc_ref).astype(BF16)

    dt = _softplus(sm_ref[...] + dtb_ref[...])
    a = dt * (-jnp.exp(alog_ref[...]))
    row = lax.broadcasted_iota(jnp.int32, (L, L), 0)
    col = lax.broadcasted_iota(jnp.int32, (L, L), 1)
    causal = row >= col
    tril = jnp.where(causal, 1.0, 0.0).astype(BF16)
    a_cum = _select_dot(tril, a)
    cumt_ref[...] = a_cum.T

    base = SMALL_DT_OFF + g * SSD_HEADS_PER_GROUP
    sel_r = lax.broadcasted_iota(jnp.int32, (LANES, gw), 0)
    sel_c = lax.broadcasted_iota(jnp.int32, (LANES, gw), 1)
    sel = jnp.where(sel_r == base + (sel_c >> HEAD_DIM_SHIFT), 1.0, 0.0).astype(BF16)
    hw = SSD_HEADS_PER_GROUP * L
    selw_r = lax.broadcasted_iota(jnp.int32, (LANES, hw), 0)
    selw_c = lax.broadcasted_iota(jnp.int32, (LANES, hw), 1)
    selw = jnp.where(selw_r == base + (selw_c >> CHUNK_SHIFT), 1.0, 0.0).astype(BF16)

    dt_x = _dot_select(dt, sel)
    cum_x = _dot_select(a_cum, sel)
    cum_w = _dot_select(a_cum, selw)
    last_x = cum_x[L - 1:L, :]

    xdt = xs * dt_x
    xdt_bf = xdt.astype(BF16)
    cb = _dot_nt(cm, bm)

    lane = lax.broadcasted_iota(jnp.int32, (L, 2 * SSD_HEAD_DIM), 1)
    first_half = lane < SSD_HEAD_DIM
    y_parts = []
    for p in range(SSD_HEADS_PER_GROUP // 2):
        xpair = xdt_bf[:, p * 2 * SSD_HEAD_DIM:(p + 1) * 2 * SSD_HEAD_DIM]
        ys = []
        for j in (2 * p, 2 * p + 1):
            colb = cum_w[:, j * L:(j + 1) * L]
            rowb = cumt_ref[pl.ds(base + j, 1), :]
            decay = jnp.where(causal, jnp.exp(colb - rowb), 0.0)
            ys.append(_dot((cb * decay).astype(BF16), xpair))
        y_parts.append(jnp.where(first_half, ys[0], ys[1]))
    y_diag = jnp.concatenate(y_parts, axis=1)

    prev = state_ref[...]
    y_off = _dot(cm, prev.astype(BF16)) * jnp.exp(cum_x)
    xw = (xdt * jnp.exp(last_x - cum_x)).astype(BF16)
    state_ref[...] = prev * jnp.exp(last_x) + _dot_tn(bm, xw)

    y = y_diag + y_off + xs * dsk_ref[...]
    yg = y * _silu(z_ref[...])
    ms = jnp.mean(yg * yg, axis=-1, keepdims=True)
    o_ref[...] = (yg * lax.rsqrt(ms + RMS_EPS) * nw_ref[...]).astype(o_ref.dtype)


def _ssd_mixer(proj, small, conv_w, conv_b, dtb_pad, alog_pad, dskip_x, norm_w):
    s = proj.shape[0]
    L, gw, n = SSD_CHUNK, SSD_GROUP_WIDTH, SSD_STATE
    xb, bb, cbk = SSD_WIDTH // gw, SSD_WIDTH // n, (SSD_WIDTH + SSD_GROUPS * n) // n
    nbytes = (2 * L * (2 * gw + 2 * n + LANES) * 4 + 2 * L * gw * 2
              + (L + CONV_PAD) * (gw + 2 * n) * 4 + LANES * L * 4 + n * gw * 4)
    return pl.pallas_call(
        _ssd_kernel,
        grid=(SSD_GROUPS, s // L),
        in_specs=[
            pl.BlockSpec((L, gw), lambda g, c: (c, OFF_X // gw + g)),
            pl.BlockSpec((L, n), lambda g, c: (c, OFF_B // n + g)),
            pl.BlockSpec((L, n), lambda g, c: (c, OFF_C // n + g)),
            pl.BlockSpec((L, gw), lambda g, c: (c, OFF_Z // gw + g)),
            pl.BlockSpec((L, LANES), lambda g, c: (c, 0)),
            pl.BlockSpec((SSD_CONV, gw), lambda g, c: (0, g)),
            pl.BlockSpec((SSD_CONV, n), lambda g, c: (0, bb + g)),
            pl.BlockSpec((SSD_CONV, n), lambda g, c: (0, cbk + g)),
            pl.BlockSpec((1, gw), lambda g, c: (0, g)),
            pl.BlockSpec((1, n), lambda g, c: (0, bb + g)),
            pl.BlockSpec((1, n), lambda g, c: (0, cbk + g)),
            pl.BlockSpec((1, LANES), lambda g, c: (0, 0)),
            pl.BlockSpec((1, LANES), lambda g, c: (0, 0)),
            pl.BlockSpec((1, gw), lambda g, c: (0, g)),
            pl.BlockSpec((1, gw), lambda g, c: (0, g)),
        ],
        out_specs=pl.BlockSpec((L, gw), lambda g, c: (c, g)),
        out_shape=jax.ShapeDtypeStruct((s, SSD_WIDTH), BF16),
        scratch_shapes=[
            pltpu.VMEM((L + CONV_PAD, gw), F32),
            pltpu.VMEM((L + CONV_PAD, n), F32),
            pltpu.VMEM((L + CONV_PAD, n), F32),
            pltpu.VMEM((LANES, L), F32),
            pltpu.VMEM((n, gw), F32),
        ],
        compiler_params=pltpu.CompilerParams(
            dimension_semantics=("arbitrary", "arbitrary"),
            vmem_limit_bytes=_vmem_limit(nbytes)),
        name="ssd_mixer",
    )(proj, proj, proj, proj, small,
      conv_w, conv_w, conv_w, conv_b, conv_b, conv_b,
      dtb_pad, alog_pad, dskip_x, norm_w.reshape(1, -1))


def kernel(x, norm_mix_w, w_in, gla_gate_w2, gla_gate_b, gla_norm_w, conv_w, conv_b,
           dt_bias, a_log, d_skip, ssd_norm_w, w_out, norm_mlp_w, w_up, w_down, norm_f_w):
    bsz, seq, d = x.shape
    h = x.reshape(bsz * seq, d)
    m = h.shape[0]
    depth = w_in.shape[0]
    gate_lo = OFF_G + GLA_WIDTH
    z_lo = gate_lo + GLA_GATE_RANK
    dt_lo = z_lo + SSD_WIDTH + SSD_XBC
    pad_heads = lambda v: jnp.pad(v, (SMALL_DT_OFF, 0)).reshape(1, LANES)

    for l in range(depth):
        w_small = jnp.concatenate(
            [w_in[l][:, gate_lo:z_lo],
             jnp.zeros((d, SMALL_DT_OFF - GLA_GATE_RANK), w_in.dtype),
             w_in[l][:, dt_lo:]], axis=1).astype(BF16)
        w2_pad = jnp.pad(gla_gate_w2[l], ((0, LANES - GLA_GATE_RANK), (0, 0))).astype(BF16)
        dskip_x = jnp.repeat(d_skip[l], SSD_HEAD_DIM).reshape(1, SSD_WIDTH)

        u = _rmsnorm(h, norm_mix_w[l], BF16)
        proj = _proj(u, w_in[l], n_out=N_MAIN, tm=1024, tn=PROJ_TN, out_dtype=F32,
                     shift=GLA_GATE_RANK, shift_from=gate_lo // PROJ_TN, name="in_proj")
        small = _matmul_small(u, w_small, tm=1024, out_dtype=F32, name="in_proj_small")

        o_gla = _gla_mixer(proj, small, w2_pad, gla_gate_b[l], gla_norm_w[l])
        o_ssd = _ssd_mixer(proj, small, conv_w[l], conv_b[l].reshape(1, -1),
                           pad_heads(dt_bias[l]), pad_heads(a_log[l]), dskip_x, ssd_norm_w[l])

        h = _matmul_res([o_gla, o_ssd], w_out[l].astype(BF16), h,
                        tm=1024, tn=1024, tk=2048, name="out_proj")
        u = _rmsnorm(h, norm_mlp_w[l], BF16)
        hid = _proj(u, w_up[l], n_out=w_up.shape[2], tm=1024, tn=PROJ_TN, out_dtype=BF16,
                    relu_sq=True, name="mlp_up")
        h = _matmul_res([hid], w_down[l].astype(BF16), h,
                        tm=1024, tn=1024, tk=2048, name="mlp_down")

    out = _rmsnorm(h, norm_f_w, x.dtype)
    return out.reshape(bsz, seq, d)
```

```python
import functools
import math

import jax
import jax.numpy as jnp
from jax import lax
from jax.experimental import pallas as pl
from jax.experimental.pallas import tpu as pltpu

F32 = jnp.float32
BF16 = jnp.bfloat16

D_MODEL = 4096
GLA_HEADS = 8
GLA_DK = 256
GLA_DV = 512
GLA_QK = GLA_HEADS * GLA_DK
GLA_WIDTH = GLA_HEADS * GLA_DV
GLA_GATE_RANK = 16
GLA_GATE_NORMALIZER = 16.0
GLA_CHUNK = 64
SSD_WIDTH = 4096
SSD_HEAD_DIM = 64
SSD_HEADS = SSD_WIDTH // SSD_HEAD_DIM
SSD_STATE = 128
SSD_GROUPS = 8
SSD_HEADS_PER_GROUP = SSD_HEADS // SSD_GROUPS
SSD_GROUP_WIDTH = SSD_HEADS_PER_GROUP * SSD_HEAD_DIM
SSD_CONV = 4
SSD_CHUNK = 128
SSD_XBC = SSD_WIDTH + 2 * SSD_GROUPS * SSD_STATE
RMS_EPS = 1e-5

LANES = 128
SUBLANES = 8
VMEM_BYTES_V7X = 64 * 1024 * 1024

OFF_Q = 0
OFF_K = OFF_Q + GLA_QK
OFF_V = OFF_K + GLA_QK
OFF_G = OFF_V + GLA_WIDTH
N_GLA = OFF_G + GLA_WIDTH
OFF_Z = 0
OFF_X = OFF_Z + SSD_WIDTH
OFF_B = OFF_X + SSD_WIDTH
OFF_C = OFF_B + SSD_GROUPS * SSD_STATE
N_SSD = OFF_C + SSD_GROUPS * SSD_STATE
SMALL_DT_OFF = LANES - SSD_HEADS


def _vmem_limit(nbytes):
    return int(min(nbytes + (8 << 20), VMEM_BYTES_V7X - (4 << 20)))


def _dot(a, b):
    return jnp.dot(a, b, preferred_element_type=F32)


def _dot_nt(a, b):
    return lax.dot_general(a, b, (((1,), (1,)), ((), ())), preferred_element_type=F32)


def _dot_tn(a, b):
    return lax.dot_general(a, b, (((0,), (0,)), ((), ())), preferred_element_type=F32)


def _split3(x):
    hi = x.astype(BF16)
    r1 = x - hi.astype(F32)
    mid = r1.astype(BF16)
    lo = (r1 - mid.astype(F32)).astype(BF16)
    return hi, mid, lo


def _select_dot(sel_bf16, x):
    hi, mid, lo = _split3(x)
    return _dot(sel_bf16, hi) + _dot(sel_bf16, mid) + _dot(sel_bf16, lo)


def _dot_select(x, sel_bf16):
    hi, mid, lo = _split3(x)
    return _dot(hi, sel_bf16) + _dot(mid, sel_bf16) + _dot(lo, sel_bf16)


def _softplus(x):
    return jnp.maximum(x, 0.0) + jnp.log1p(jnp.exp(-jnp.abs(x)))


def _log_sigmoid(x):
    return jnp.minimum(x, 0.0) - jnp.log1p(jnp.exp(-jnp.abs(x)))


def _silu(x):
    return x * jax.nn.sigmoid(x)


def _rmsnorm_kernel(x_ref, w_ref, o_ref):
    x = x_ref[...]
    ms = jnp.mean(x * x, axis=-1, keepdims=True)
    o_ref[...] = (x * lax.rsqrt(ms + RMS_EPS) * w_ref[...]).astype(o_ref.dtype)


def _rmsnorm(x, w, out_dtype, tm=256):
    m, d = x.shape
    nbytes = 2 * tm * d * (4 + jnp.dtype(out_dtype).itemsize) + 2 * d * 4
    return pl.pallas_call(
        _rmsnorm_kernel,
        grid=(m // tm,),
        in_specs=[pl.BlockSpec((tm, d), lambda i: (i, 0)),
                  pl.BlockSpec((1, d), lambda i: (0, 0))],
        out_specs=pl.BlockSpec((tm, d), lambda i: (i, 0)),
        out_shape=jax.ShapeDtypeStruct((m, d), out_dtype),
        compiler_params=pltpu.CompilerParams(
            dimension_semantics=("arbitrary",), vmem_limit_bytes=_vmem_limit(nbytes)),
        name="rmsnorm",
    )(x, w.reshape(1, d))


PROJ_TM = 512
PROJ_TN = 1024


def _proj_kernel(*refs, shift, relu_sq):
    if shift:
        a_ref, w_ref, wx_ref, o_ref = refs
        w = jnp.concatenate([w_ref[:, shift:], wx_ref[:, :shift]], axis=1)
    else:
        a_ref, w_ref, o_ref = refs
        w = w_ref[...]
    acc = _dot(a_ref[...], w.astype(BF16))
    if relu_sq:
        acc = jnp.square(jnp.maximum(acc, 0.0))
    o_ref[...] = acc.astype(o_ref.dtype)


def _proj(a, w, *, col0, n_out, tm, tn, out_dtype, relu_sq=False, name):
    m, kdim = a.shape
    blk0, shift = divmod(col0, LANES)
    per = tn // LANES
    assert blk0 % per == 0 and n_out % tn == 0 and m % tm == 0
    in_specs = [pl.BlockSpec((tm, kdim), lambda j, i: (i, 0)),
                pl.BlockSpec((kdim, tn), lambda j, i: (0, blk0 // per + j))]
    args = [a, w]
    if shift:
        in_specs.append(pl.BlockSpec((kdim, LANES), lambda j, i: (0, blk0 + (j + 1) * per)))
        args.append(w)
    nbytes = (2 * tm * kdim * 2 + 2 * kdim * (tn + LANES * bool(shift)) * 4 + kdim * tn * 2
              + 2 * tm * tn * jnp.dtype(out_dtype).itemsize + tm * tn * 4)
    return pl.pallas_call(
        functools.partial(_proj_kernel, shift=shift, relu_sq=relu_sq),
        grid=(n_out // tn, m // tm),
        in_specs=in_specs,
        out_specs=pl.BlockSpec((tm, tn), lambda j, i: (i, j)),
        out_shape=jax.ShapeDtypeStruct((m, n_out), out_dtype),
        compiler_params=pltpu.CompilerParams(
            dimension_semantics=("arbitrary", "arbitrary"),
            vmem_limit_bytes=_vmem_limit(nbytes)),
        name=name,
    )(*args)


def _matmul_small(a, b, *, tm, out_dtype, name):
    m, kdim = a.shape
    n = b.shape[1]

    def body(a_ref, b_ref, o_ref):
        o_ref[...] = _dot(a_ref[...], b_ref[...].astype(BF16)).astype(o_ref.dtype)

    nbytes = 2 * tm * kdim * 2 + 3 * kdim * n * 4 + 3 * tm * n * 4
    return pl.pallas_call(
        body,
        grid=(m // tm,),
        in_specs=[pl.BlockSpec((tm, kdim), lambda i: (i, 0)),
                  pl.BlockSpec((kdim, n), lambda i: (0, 0))],
        out_specs=pl.BlockSpec((tm, n), lambda i: (i, 0)),
        out_shape=jax.ShapeDtypeStruct((m, n), out_dtype),
        compiler_params=pltpu.CompilerParams(
            dimension_semantics=("arbitrary",), vmem_limit_bytes=_vmem_limit(nbytes)),
        name=name,
    )(a, b)


def _matmul_res_kernel(*refs, nk, n_lhs):
    a_refs = refs[:n_lhs]
    b_ref, res_ref, o_ref, acc_ref = refs[n_lhs:]
    k = pl.program_id(2)

    @pl.when(k == 0)
    def _():
        acc_ref[...] = jnp.zeros_like(acc_ref)

    if n_lhs == 1:
        a = a_refs[0][...]
    else:
        a = jnp.where(k < nk // 2, a_refs[0][...], a_refs[1][...])
    acc_ref[...] += _dot(a, b_ref[...])

    @pl.when(k == nk - 1)
    def _():
        o_ref[...] = (acc_ref[...] + res_ref[...]).astype(o_ref.dtype)


def _matmul_res(lhs, b, residual, *, tm, tn, tk, name):
    n_lhs = len(lhs)
    m = lhs[0].shape[0]
    kdim, n = b.shape
    nk = kdim // tk
    half = nk // 2
    if n_lhs == 1:
        in_specs = [pl.BlockSpec((tm, tk), lambda j, i, k: (i, k))]
    else:
        in_specs = [pl.BlockSpec((tm, tk), lambda j, i, k: (i, jnp.minimum(k, half - 1))),
                    pl.BlockSpec((tm, tk), lambda j, i, k: (i, jnp.maximum(k - half, 0)))]
    in_specs += [pl.BlockSpec((tk, tn), lambda j, i, k: (k, j)),
                 pl.BlockSpec((tm, tn), lambda j, i, k: (i, j))]
    nbytes = 2 * (n_lhs * tm * tk + tk * tn) * 2 + 4 * tm * tn * 4 + 2 * tm * tn * 4
    return pl.pallas_call(
        functools.partial(_matmul_res_kernel, nk=nk, n_lhs=n_lhs),
        grid=(n // tn, m // tm, nk),
        in_specs=in_specs,
        out_specs=pl.BlockSpec((tm, tn), lambda j, i, k: (i, j)),
        out_shape=jax.ShapeDtypeStruct((m, n), residual.dtype),
        scratch_shapes=[pltpu.VMEM((tm, tn), F32)],
        compiler_params=pltpu.CompilerParams(
            dimension_semantics=("arbitrary", "arbitrary", "arbitrary"),
            vmem_limit_bytes=_vmem_limit(nbytes)),
        name=name,
    )(*lhs, b, residual)


GLA_BLOCK = 512


def _gla_kernel(q_ref, k_ref, v_ref, g_ref, sm_ref, w2_ref, gb_ref, nw_ref,
                o_ref, state_ref):
    c = GLA_CHUNK

    @pl.when(pl.program_id(1) == 0)
    def _():
        state_ref[...] = jnp.zeros_like(state_ref)

    row = lax.broadcasted_iota(jnp.int32, (c, c), 0)
    col = lax.broadcasted_iota(jnp.int32, (c, c), 1)
    causal = row >= col
    tril = jnp.where(causal, 1.0, 0.0).astype(BF16)

    pre = _dot(sm_ref[...].astype(BF16), w2_ref[...]) + gb_ref[...]
    log_a = _log_sigmoid(pre) * (1.0 / GLA_GATE_NORMALIZER)

    for n in range(GLA_BLOCK // c):
        rows = slice(n * c, (n + 1) * c)
        b = _select_dot(tril, log_a[rows, :])
        b_last = b[c - 1:c, :]
        qn = q_ref[rows, :] * (GLA_DK ** -0.5)
        kn = k_ref[rows, :]
        vn = v_ref[rows, :].astype(BF16)
        q_dec = (qn * jnp.exp(b)).astype(BF16)
        k_dec = (kn * jnp.exp(-b)).astype(BF16)
        k_end = (kn * jnp.exp(b_last - b)).astype(BF16)
        scores = jnp.where(causal, _dot_nt(q_dec, k_dec), 0.0)
        state = state_ref[...]
        o = _dot(scores.astype(BF16), vn) + _dot_nt(q_dec, state.astype(BF16))
        state_ref[...] = state * jnp.exp(b_last) + _dot_tn(vn, k_end)
        ms = jnp.mean(o * o, axis=-1, keepdims=True)
        o = o * lax.rsqrt(ms + RMS_EPS) * nw_ref[...]
        o_ref[rows, :] = (o * _silu(g_ref[rows, :])).astype(o_ref.dtype)


def _gla_mixer(proj, small, w2_pad, gate_b, norm_w):
    s = proj.shape[0]
    t = GLA_BLOCK
    dkb, dvb = GLA_DK, GLA_DV
    nbytes = 2 * t * (2 * dkb + 2 * dvb + LANES) * 4 + 2 * t * dvb * 2 + dvb * dkb * 4
    return pl.pallas_call(
        _gla_kernel,
        grid=(GLA_HEADS, s // t),
        in_specs=[
            pl.BlockSpec((t, dkb), lambda h, i: (i, OFF_Q // dkb + h)),
            pl.BlockSpec((t, dkb), lambda h, i: (i, OFF_K // dkb + h)),
            pl.BlockSpec((t, dvb), lambda h, i: (i, OFF_V // dvb + h)),
            pl.BlockSpec((t, dvb), lambda h, i: (i, OFF_G // dvb + h)),
            pl.BlockSpec((t, LANES), lambda h, i: (i, 0)),
            pl.BlockSpec((LANES, dkb), lambda h, i: (0, h)),
            pl.BlockSpec((1, dkb), lambda h, i: (0, h)),
            pl.BlockSpec((1, dvb), lambda h, i: (0, 0)),
        ],
        out_specs=pl.BlockSpec((t, dvb), lambda h, i: (i, h)),
        out_shape=jax.ShapeDtypeStruct((s, GLA_WIDTH), BF16),
        scratch_shapes=[pltpu.VMEM((dvb, dkb), F32)],
        compiler_params=pltpu.CompilerParams(
            dimension_semantics=("arbitrary", "arbitrary"),
            vmem_limit_bytes=_vmem_limit(nbytes)),
        name="gla_mixer",
    )(proj, proj, proj, proj, small, w2_pad, gate_b.reshape(1, -1), norm_w.reshape(1, -1))


CONV_PAD = SUBLANES
HEAD_DIM_SHIFT = SSD_HEAD_DIM.bit_length() - 1
CHUNK_SHIFT = SSD_CHUNK.bit_length() - 1


def _ssd_kernel(xr_ref, br_ref, cr_ref, z_ref, sm_ref,
                cwx_ref, cwb_ref, cwc_ref, cbx_ref, cbb_ref, cbc_ref,
                dtb_ref, alog_ref, dsk_ref, nw_ref,
                o_ref, xpad_ref, bpad_ref, cpad_ref, cumt_ref, state_ref):
    L = SSD_CHUNK
    gw = SSD_GROUP_WIDTH
    g = pl.program_id(0)

    @pl.when(pl.program_id(1) == 0)
    def _():
        xpad_ref[0:CONV_PAD, :] = jnp.zeros((CONV_PAD, gw), F32)
        bpad_ref[0:CONV_PAD, :] = jnp.zeros((CONV_PAD, SSD_STATE), F32)
        cpad_ref[0:CONV_PAD, :] = jnp.zeros((CONV_PAD, SSD_STATE), F32)
        state_ref[...] = jnp.zeros_like(state_ref)

    def conv_silu(raw_ref, pad_ref, w_ref, b_ref):
        pad_ref[CONV_PAD:CONV_PAD + L, :] = raw_ref[...]
        acc = b_ref[...]
        for k in range(SSD_CONV):
            acc = acc + w_ref[k:k + 1, :] * pad_ref[pl.ds(CONV_PAD - (SSD_CONV - 1) + k, L), :]
        pad_ref[0:CONV_PAD, :] = pad_ref[L:L + CONV_PAD, :]
        return _silu(acc)

    xs = conv_silu(xr_ref, xpad_ref, cwx_ref, cbx_ref)
    bm = conv_silu(br_ref, bpad_ref, cwb_ref, cbb_ref).astype(BF16)
    cm = conv_silu(cr_ref, cpad_ref, cwc_ref, cbc_ref).astype(BF16)

    dt = _softplus(sm_ref[...] + dtb_ref[...])
    a = dt * (-jnp.exp(alog_ref[...]))
    row = lax.broadcasted_iota(jnp.int32, (L, L), 0)
    col = lax.broadcasted_iota(jnp.int32, (L, L), 1)
    causal = row >= col
    tril = jnp.where(causal, 1.0, 0.0).astype(BF16)
    a_cum = _select_dot(tril, a)
    cumt_ref[...] = a_cum.T

    base = SMALL_DT_OFF + g * SSD_HEADS_PER_GROUP
    sel_r = lax.broadcasted_iota(jnp.int32, (LANES, gw), 0)
    sel_c = lax.broadcasted_iota(jnp.int32, (LANES, gw), 1)
    sel = jnp.where(sel_r == base + (sel_c >> HEAD_DIM_SHIFT), 1.0, 0.0).astype(BF16)
    hw = SSD_HEADS_PER_GROUP * L
    selw_r = lax.broadcasted_iota(jnp.int32, (LANES, hw), 0)
    selw_c = lax.broadcasted_iota(jnp.int32, (LANES, hw), 1)
    selw = jnp.where(selw_r == base + (selw_c >> CHUNK_SHIFT), 1.0, 0.0).astype(BF16)

    dt_x = _dot_select(dt, sel)
    cum_x = _dot_select(a_cum, sel)
    cum_w = _dot_select(a_cum, selw)
    last_x = cum_x[L - 1:L, :]

    xdt = xs * dt_x
    xdt_bf = xdt.astype(BF16)
    cb = _dot_nt(cm, bm)

    lane = lax.broadcasted_iota(jnp.int32, (L, 2 * SSD_HEAD_DIM), 1)
    first_half = lane < SSD_HEAD_DIM
    y_parts = []
    for p in range(SSD_HEADS_PER_GROUP // 2):
        xpair = xdt_bf[:, p * 2 * SSD_HEAD_DIM:(p + 1) * 2 * SSD_HEAD_DIM]
        ys = []
        for j in (2 * p, 2 * p + 1):
            colb = cum_w[:, j * L:(j + 1) * L]
            rowb = cumt_ref[pl.ds(base + j, 1), :]
            decay = jnp.where(causal, jnp.exp(colb - rowb), 0.0)
            ys.append(_dot((cb * decay).astype(BF16), xpair))
        y_parts.append(jnp.where(first_half, ys[0], ys[1]))
    y_diag = jnp.concatenate(y_parts, axis=1)

    prev = state_ref[...]
    y_off = _dot(cm, prev.astype(BF16)) * jnp.exp(cum_x)
    xw = (xdt * jnp.exp(last_x - cum_x)).astype(BF16)
    state_ref[...] = prev * jnp.exp(last_x) + _dot_tn(bm, xw)

    y = y_diag + y_off + xs * dsk_ref[...]
    yg = y * _silu(z_ref[...])
    ms = jnp.mean(yg * yg, axis=-1, keepdims=True)
    o_ref[...] = (yg * lax.rsqrt(ms + RMS_EPS) * nw_ref[...]).astype(o_ref.dtype)


def _ssd_mixer(proj, small, conv_w, conv_b, dtb_pad, alog_pad, dskip_x, norm_w):
    s = proj.shape[0]
    L, gw, n = SSD_CHUNK, SSD_GROUP_WIDTH, SSD_STATE
    xb, bb, cbk = SSD_WIDTH // gw, SSD_WIDTH // n, (SSD_WIDTH + SSD_GROUPS * n) // n
    nbytes = (2 * L * (2 * gw + 2 * n + LANES) * 4 + 2 * L * gw * 2
              + (L + CONV_PAD) * (gw + 2 * n) * 4 + LANES * L * 4 + n * gw * 4)
    return pl.pallas_call(
        _ssd_kernel,
        grid=(SSD_GROUPS, s // L),
        in_specs=[
            pl.BlockSpec((L, gw), lambda g, c: (c, OFF_X // gw + g)),
            pl.BlockSpec((L, n), lambda g, c: (c, OFF_B // n + g)),
            pl.BlockSpec((L, n), lambda g, c: (c, OFF_C // n + g)),
            pl.BlockSpec((L, gw), lambda g, c: (c, OFF_Z // gw + g)),
            pl.BlockSpec((L, LANES), lambda g, c: (c, 0)),
            pl.BlockSpec((SSD_CONV, gw), lambda g, c: (0, g)),
            pl.BlockSpec((SSD_CONV, n), lambda g, c: (0, bb + g)),
            pl.BlockSpec((SSD_CONV, n), lambda g, c: (0, cbk + g)),
            pl.BlockSpec((1, gw), lambda g, c: (0, g)),
            pl.BlockSpec((1, n), lambda g, c: (0, bb + g)),
            pl.BlockSpec((1, n), lambda g, c: (0, cbk + g)),
            pl.BlockSpec((1, LANES), lambda g, c: (0, 0)),
            pl.BlockSpec((1, LANES), lambda g, c: (0, 0)),
            pl.BlockSpec((1, gw), lambda g, c: (0, g)),
            pl.BlockSpec((1, gw), lambda g, c: (0, g)),
        ],
        out_specs=pl.BlockSpec((L, gw), lambda g, c: (c, g)),
        out_shape=jax.ShapeDtypeStruct((s, SSD_WIDTH), BF16),
        scratch_shapes=[
            pltpu.VMEM((L + CONV_PAD, gw), F32),
            pltpu.VMEM((L + CONV_PAD, n), F32),
            pltpu.VMEM((L + CONV_PAD, n), F32),
            pltpu.VMEM((LANES, L), F32),
            pltpu.VMEM((n, gw), F32),
        ],
        compiler_params=pltpu.CompilerParams(
            dimension_semantics=("arbitrary", "arbitrary"),
            vmem_limit_bytes=_vmem_limit(nbytes)),
        name="ssd_mixer",
    )(proj, proj, proj, proj, small,
      conv_w, conv_w, conv_w, conv_b, conv_b, conv_b,
      dtb_pad, alog_pad, dskip_x, norm_w.reshape(1, -1))


def kernel(x, norm_mix_w, w_in, gla_gate_w2, gla_gate_b, gla_norm_w, conv_w, conv_b,
           dt_bias, a_log, d_skip, ssd_norm_w, w_out, norm_mlp_w, w_up, w_down, norm_f_w):
    bsz, seq, d = x.shape
    h = x.reshape(bsz * seq, d)
    m = h.shape[0]
    depth = w_in.shape[0]
    gate_lo = N_GLA
    z_lo = gate_lo + GLA_GATE_RANK
    dt_lo = z_lo + SSD_WIDTH + SSD_XBC
    pad_heads = lambda v: jnp.pad(v, (SMALL_DT_OFF, 0)).reshape(1, LANES)

    for l in range(depth):
        w_small = jnp.concatenate(
            [lax.slice_in_dim(w_in[l], gate_lo, z_lo, axis=1),
             jnp.zeros((d, SMALL_DT_OFF - GLA_GATE_RANK), w_in.dtype),
             lax.slice_in_dim(w_in[l], dt_lo, dt_lo + SSD_HEADS, axis=1)], axis=1)
        w2_pad = jnp.pad(gla_gate_w2[l], ((0, LANES - GLA_GATE_RANK), (0, 0))).astype(BF16)
        dskip_x = jnp.repeat(d_skip[l], SSD_HEAD_DIM).reshape(1, SSD_WIDTH)

        u = _rmsnorm(h, norm_mix_w[l], BF16)
        proj_gla = _proj(u, w_in[l], col0=0, n_out=N_GLA, tm=PROJ_TM, tn=PROJ_TN,
                         out_dtype=F32, name="in_proj_gla")
        proj_ssd = _proj(u, w_in[l], col0=z_lo, n_out=N_SSD, tm=PROJ_TM, tn=PROJ_TN,
                         out_dtype=F32, name="in_proj_ssd")
        small = _matmul_small(u, w_small, tm=1024, out_dtype=F32, name="in_proj_small")

        o_gla = _gla_mixer(proj_gla, small, w2_pad, gla_gate_b[l], gla_norm_w[l])
        o_ssd = _ssd_mixer(proj_ssd, small, conv_w[l], conv_b[l].reshape(1, -1),
                           pad_heads(dt_bias[l]), pad_heads(a_log[l]), dskip_x, ssd_norm_w[l])

        h = _matmul_res([o_gla, o_ssd], w_out[l].astype(BF16), h,
                        tm=1024, tn=1024, tk=2048, name="out_proj")
        u = _rmsnorm(h, norm_mlp_w[l], BF16)
        hid = _proj(u, w_up[l], col0=0, n_out=w_up.shape[2], tm=PROJ_TM, tn=PROJ_TN,
                    out_dtype=BF16, relu_sq=True, name="mlp_up")
        h = _matmul_res([hid], w_down[l].astype(BF16), h,
                        tm=1024, tn=1024, tk=2048, name="mlp_down")

    out = _rmsnorm(h, norm_f_w, x.dtype)
    return out.reshape(bsz, seq, d)
```

```python
import functools
import math

import jax
import jax.numpy as jnp
from jax import lax
from jax.experimental import pallas as pl
from jax.experimental.pallas import tpu as pltpu

F32 = jnp.float32
BF16 = jnp.bfloat16

D_MODEL = 4096
GLA_HEADS = 8
GLA_DK = 256
GLA_DV = 512
GLA_QK = GLA_HEADS * GLA_DK
GLA_WIDTH = GLA_HEADS * GLA_DV
GLA_GATE_RANK = 16
GLA_GATE_NORMALIZER = 16.0
GLA_CHUNK = 64
SSD_WIDTH = 4096
SSD_HEAD_DIM = 64
SSD_HEADS = SSD_WIDTH // SSD_HEAD_DIM
SSD_STATE = 128
SSD_GROUPS = 8
SSD_HEADS_PER_GROUP = SSD_HEADS // SSD_GROUPS
SSD_GROUP_WIDTH = SSD_HEADS_PER_GROUP * SSD_HEAD_DIM
SSD_CONV = 4
SSD_CHUNK = 128
SSD_XBC = SSD_WIDTH + 2 * SSD_GROUPS * SSD_STATE
RMS_EPS = 1e-5

LANES = 128
SUBLANES = 8
VMEM_BYTES_V7X = 64 * 1024 * 1024

OFF_Q = 0
OFF_K = OFF_Q + GLA_QK
OFF_V = OFF_K + GLA_QK
OFF_G = OFF_V + GLA_WIDTH
N_GLA = OFF_G + GLA_WIDTH
OFF_Z = 0
OFF_X = OFF_Z + SSD_WIDTH
OFF_B = OFF_X + SSD_WIDTH
OFF_C = OFF_B + SSD_GROUPS * SSD_STATE
N_SSD = OFF_C + SSD_GROUPS * SSD_STATE
SMALL_DT_OFF = LANES - SSD_HEADS


def _vmem_limit(nbytes):
    return int(min(nbytes + (8 << 20), VMEM_BYTES_V7X - (4 << 20)))


def _dot(a, b):
    return jnp.dot(a, b, preferred_element_type=F32)


def _dot_nt(a, b):
    return lax.dot_general(a, b, (((1,), (1,)), ((), ())), preferred_element_type=F32)


def _dot_tn(a, b):
    return lax.dot_general(a, b, (((0,), (0,)), ((), ())), preferred_element_type=F32)


def _split3(x):
    hi = x.astype(BF16)
    r1 = x - hi.astype(F32)
    mid = r1.astype(BF16)
    lo = (r1 - mid.astype(F32)).astype(BF16)
    return hi, mid, lo


def _select_dot(sel_bf16, x):
    hi, mid, lo = _split3(x)
    return _dot(sel_bf16, hi) + _dot(sel_bf16, mid) + _dot(sel_bf16, lo)


def _dot_select(x, sel_bf16):
    hi, mid, lo = _split3(x)
    return _dot(hi, sel_bf16) + _dot(mid, sel_bf16) + _dot(lo, sel_bf16)


def _softplus(x):
    return jnp.maximum(x, 0.0) + jnp.log1p(jnp.exp(-jnp.abs(x)))


def _log_sigmoid(x):
    return jnp.minimum(x, 0.0) - jnp.log1p(jnp.exp(-jnp.abs(x)))


def _silu(x):
    return x * jax.nn.sigmoid(x)


def _rmsnorm_kernel(x_ref, w_ref, o_ref):
    x = x_ref[...]
    ms = jnp.mean(x * x, axis=-1, keepdims=True)
    o_ref[...] = (x * lax.rsqrt(ms + RMS_EPS) * w_ref[...]).astype(o_ref.dtype)


def _rmsnorm(x, w, out_dtype, tm=256):
    m, d = x.shape
    nbytes = 2 * tm * d * (4 + jnp.dtype(out_dtype).itemsize) + 2 * d * 4
    return pl.pallas_call(
        _rmsnorm_kernel,
        grid=(m // tm,),
        in_specs=[pl.BlockSpec((tm, d), lambda i: (i, 0)),
                  pl.BlockSpec((1, d), lambda i: (0, 0))],
        out_specs=pl.BlockSpec((tm, d), lambda i: (i, 0)),
        out_shape=jax.ShapeDtypeStruct((m, d), out_dtype),
        compiler_params=pltpu.CompilerParams(
            dimension_semantics=("arbitrary",), vmem_limit_bytes=_vmem_limit(nbytes)),
        name="rmsnorm",
    )(x, w.reshape(1, d))


PROJ_TM = 512
PROJ_TN = 1024


def _proj_kernel(a_ref, w_ref, o_ref, *, w_is_transposed, relu_sq):
    w = w_ref[...].astype(BF16)
    acc = _dot_nt(a_ref[...], w) if w_is_transposed else _dot(a_ref[...], w)
    if relu_sq:
        acc = jnp.square(jnp.maximum(acc, 0.0))
    o_ref[...] = acc.astype(o_ref.dtype)


def _proj(a, w, *, n_out, tm, tn, out_dtype, row0=None, relu_sq=False, name):
    m, kdim = a.shape
    assert n_out % tn == 0 and m % tm == 0
    if row0 is None:
        w_spec = pl.BlockSpec((kdim, tn), lambda j, i: (0, j))
    else:
        assert row0 % SUBLANES == 0
        w_spec = pl.BlockSpec((pl.Element(tn), pl.Element(kdim)),
                              lambda j, i: (pl.multiple_of(row0 + j * tn, SUBLANES), 0))
    nbytes = (2 * tm * kdim * 2 + 2 * kdim * tn * 4 + kdim * tn * 2
              + 2 * tm * tn * jnp.dtype(out_dtype).itemsize + tm * tn * 4)
    return pl.pallas_call(
        functools.partial(_proj_kernel, w_is_transposed=row0 is not None, relu_sq=relu_sq),
        grid=(n_out // tn, m // tm),
        in_specs=[pl.BlockSpec((tm, kdim), lambda j, i: (i, 0)), w_spec],
        out_specs=pl.BlockSpec((tm, tn), lambda j, i: (i, j)),
        out_shape=jax.ShapeDtypeStruct((m, n_out), out_dtype),
        compiler_params=pltpu.CompilerParams(
            dimension_semantics=("arbitrary", "arbitrary"),
            vmem_limit_bytes=_vmem_limit(nbytes)),
        name=name,
    )(a, w)


def _matmul_small(a, b_t, *, tm, out_dtype, name):
    m, kdim = a.shape
    n = b_t.shape[0]

    def body(a_ref, b_ref, o_ref):
        o_ref[...] = _dot_nt(a_ref[...], b_ref[...].astype(BF16)).astype(o_ref.dtype)

    nbytes = 2 * tm * kdim * 2 + 3 * kdim * n * 4 + 3 * tm * n * 4
    return pl.pallas_call(
        body,
        grid=(m // tm,),
        in_specs=[pl.BlockSpec((tm, kdim), lambda i: (i, 0)),
                  pl.BlockSpec((n, kdim), lambda i: (0, 0))],
        out_specs=pl.BlockSpec((tm, n), lambda i: (i, 0)),
        out_shape=jax.ShapeDtypeStruct((m, n), out_dtype),
        compiler_params=pltpu.CompilerParams(
            dimension_semantics=("arbitrary",), vmem_limit_bytes=_vmem_limit(nbytes)),
        name=name,
    )(a, b_t)


def _matmul_res_kernel(*refs, nk, n_lhs):
    a_refs = refs[:n_lhs]
    b_ref, res_ref, o_ref, acc_ref = refs[n_lhs:]
    k = pl.program_id(2)

    @pl.when(k == 0)
    def _():
        acc_ref[...] = jnp.zeros_like(acc_ref)

    if n_lhs == 1:
        a = a_refs[0][...]
    else:
        a = jnp.where(k < nk // 2, a_refs[0][...], a_refs[1][...])
    acc_ref[...] += _dot(a, b_ref[...])

    @pl.when(k == nk - 1)
    def _():
        o_ref[...] = (acc_ref[...] + res_ref[...]).astype(o_ref.dtype)


def _matmul_res(lhs, b, residual, *, tm, tn, tk, name):
    n_lhs = len(lhs)
    m = lhs[0].shape[0]
    kdim, n = b.shape
    nk = kdim // tk
    half = nk // 2
    if n_lhs == 1:
        in_specs = [pl.BlockSpec((tm, tk), lambda j, i, k: (i, k))]
    else:
        in_specs = [pl.BlockSpec((tm, tk), lambda j, i, k: (i, jnp.minimum(k, half - 1))),
                    pl.BlockSpec((tm, tk), lambda j, i, k: (i, jnp.maximum(k - half, 0)))]
    in_specs += [pl.BlockSpec((tk, tn), lambda j, i, k: (k, j)),
                 pl.BlockSpec((tm, tn), lambda j, i, k: (i, j))]
    nbytes = 2 * (n_lhs * tm * tk + tk * tn) * 2 + 4 * tm * tn * 4 + 2 * tm * tn * 4
    return pl.pallas_call(
        functools.partial(_matmul_res_kernel, nk=nk, n_lhs=n_lhs),
        grid=(n // tn, m // tm, nk),
        in_specs=in_specs,
        out_specs=pl.BlockSpec((tm, tn), lambda j, i, k: (i, j)),
        out_shape=jax.ShapeDtypeStruct((m, n), residual.dtype),
        scratch_shapes=[pltpu.VMEM((tm, tn), F32)],
        compiler_params=pltpu.CompilerParams(
            dimension_semantics=("arbitrary", "arbitrary", "arbitrary"),
            vmem_limit_bytes=_vmem_limit(nbytes)),
        name=name,
    )(*lhs, b, residual)


GLA_BLOCK = 512


def _gla_kernel(q_ref, k_ref, v_ref, g_ref, sm_ref, w2_ref, gb_ref, nw_ref,
                o_ref, state_ref):
    c = GLA_CHUNK

    @pl.when(pl.program_id(1) == 0)
    def _():
        state_ref[...] = jnp.zeros_like(state_ref)

    row = lax.broadcasted_iota(jnp.int32, (c, c), 0)
    col = lax.broadcasted_iota(jnp.int32, (c, c), 1)
    causal = row >= col
    tril = jnp.where(causal, 1.0, 0.0).astype(BF16)

    pre = _dot(sm_ref[...].astype(BF16), w2_ref[...]) + gb_ref[...]
    log_a = _log_sigmoid(pre) * (1.0 / GLA_GATE_NORMALIZER)

    for n in range(GLA_BLOCK // c):
        rows = slice(n * c, (n + 1) * c)
        b = _select_dot(tril, log_a[rows, :])
        b_last = b[c - 1:c, :]
        qn = q_ref[rows, :] * (GLA_DK ** -0.5)
        kn = k_ref[rows, :]
        vn = v_ref[rows, :].astype(BF16)
        q_dec = (qn * jnp.exp(b)).astype(BF16)
        k_dec = (kn * jnp.exp(-b)).astype(BF16)
        k_end = (kn * jnp.exp(b_last - b)).astype(BF16)
        scores = jnp.where(causal, _dot_nt(q_dec, k_dec), 0.0)
        state = state_ref[...]
        o = _dot(scores.astype(BF16), vn) + _dot_nt(q_dec, state.astype(BF16))
        state_ref[...] = state * jnp.exp(b_last) + _dot_tn(vn, k_end)
        ms = jnp.mean(o * o, axis=-1, keepdims=True)
        o = o * lax.rsqrt(ms + RMS_EPS) * nw_ref[...]
        o_ref[rows, :] = (o * _silu(g_ref[rows, :])).astype(o_ref.dtype)


def _gla_mixer(proj, small, w2_pad, gate_b, norm_w):
    s = proj.shape[0]
    t = GLA_BLOCK
    dkb, dvb = GLA_DK, GLA_DV
    nbytes = 2 * t * (2 * dkb + 2 * dvb + LANES) * 4 + 2 * t * dvb * 2 + dvb * dkb * 4
    return pl.pallas_call(
        _gla_kernel,
        grid=(GLA_HEADS, s // t),
        in_specs=[
            pl.BlockSpec((t, dkb), lambda h, i: (i, OFF_Q // dkb + h)),
            pl.BlockSpec((t, dkb), lambda h, i: (i, OFF_K // dkb + h)),
            pl.BlockSpec((t, dvb), lambda h, i: (i, OFF_V // dvb + h)),
            pl.BlockSpec((t, dvb), lambda h, i: (i, OFF_G // dvb + h)),
            pl.BlockSpec((t, LANES), lambda h, i: (i, 0)),
            pl.BlockSpec((LANES, dkb), lambda h, i: (0, h)),
            pl.BlockSpec((1, dkb), lambda h, i: (0, h)),
            pl.BlockSpec((1, dvb), lambda h, i: (0, 0)),
        ],
        out_specs=pl.BlockSpec((t, dvb), lambda h, i: (i, h)),
        out_shape=jax.ShapeDtypeStruct((s, GLA_WIDTH), BF16),
        scratch_shapes=[pltpu.VMEM((dvb, dkb), F32)],
        compiler_params=pltpu.CompilerParams(
            dimension_semantics=("arbitrary", "arbitrary"),
            vmem_limit_bytes=_vmem_limit(nbytes)),
        name="gla_mixer",
    )(proj, proj, proj, proj, small, w2_pad, gate_b.reshape(1, -1), norm_w.reshape(1, -1))


CONV_PAD = SUBLANES
HEAD_DIM_SHIFT = SSD_HEAD_DIM.bit_length() - 1
CHUNK_SHIFT = SSD_CHUNK.bit_length() - 1


def _ssd_kernel(xr_ref, br_ref, cr_ref, z_ref, sm_ref,
                cwx_ref, cwb_ref, cwc_ref, cbx_ref, cbb_ref, cbc_ref,
                dtb_ref, alog_ref, dsk_ref, nw_ref,
                o_ref, xpad_ref, bpad_ref, cpad_ref, cumt_ref, state_ref):
    L = SSD_CHUNK
    gw = SSD_GROUP_WIDTH
    g = pl.program_id(0)

    @pl.when(pl.program_id(1) == 0)
    def _():
        xpad_ref[0:CONV_PAD, :] = jnp.zeros((CONV_PAD, gw), F32)
        bpad_ref[0:CONV_PAD, :] = jnp.zeros((CONV_PAD, SSD_STATE), F32)
        cpad_ref[0:CONV_PAD, :] = jnp.zeros((CONV_PAD, SSD_STATE), F32)
        state_ref[...] = jnp.zeros_like(state_ref)

    def conv_silu(raw_ref, pad_ref, w_ref, b_ref):
        pad_ref[CONV_PAD:CONV_PAD + L, :] = raw_ref[...]
        acc = b_ref[...]
        for k in range(SSD_CONV):
            acc = acc + w_ref[k:k + 1, :] * pad_ref[pl.ds(CONV_PAD - (SSD_CONV - 1) + k, L), :]
        pad_ref[0:CONV_PAD, :] = pad_ref[L:L + CONV_PAD, :]
        return _silu(acc)

    xs = conv_silu(xr_ref, xpad_ref, cwx_ref, cbx_ref)
    bm = conv_silu(br_ref, bpad_ref, cwb_ref, cbb_ref).astype(BF16)
    cm = conv_silu(cr_ref, cpad_ref, cwc_ref, cbc_ref).astype(BF16)

    dt = _softplus(sm_ref[...] + dtb_ref[...])
    a = dt * (-jnp.exp(alog_ref[...]))
    row = lax.broadcasted_iota(jnp.int32, (L, L), 0)
    col = lax.broadcasted_iota(jnp.int32, (L, L), 1)
    causal = row >= col
    tril = jnp.where(causal, 1.0, 0.0).astype(BF16)
    a_cum = _select_dot(tril, a)
    cumt_ref[...] = a_cum.T

    base = SMALL_DT_OFF + g * SSD_HEADS_PER_GROUP
    sel_r = lax.broadcasted_iota(jnp.int32, (LANES, gw), 0)
    sel_c = lax.broadcasted_iota(jnp.int32, (LANES, gw), 1)
    sel = jnp.where(sel_r == base + (sel_c >> HEAD_DIM_SHIFT), 1.0, 0.0).astype(BF16)
    hw = SSD_HEADS_PER_GROUP * L
    selw_r = lax.broadcasted_iota(jnp.int32, (LANES, hw), 0)
    selw_c = lax.broadcasted_iota(jnp.int32, (LANES, hw), 1)
    selw = jnp.where(selw_r == base + (selw_c >> CHUNK_SHIFT), 1.0, 0.0).astype(BF16)

    dt_x = _dot_select(dt, sel)
    cum_x = _dot_select(a_cum, sel)
    cum_w = _dot_select(a_cum, selw)
    last_x = cum_x[L - 1:L, :]

    xdt = xs * dt_x
    xdt_bf = xdt.astype(BF16)
    cb = _dot_nt(cm, bm)

    lane = lax.broadcasted_iota(jnp.int32, (L, 2 * SSD_HEAD_DIM), 1)
    first_half = lane < SSD_HEAD_DIM
    y_parts = []
    for p in range(SSD_HEADS_PER_GROUP // 2):
        xpair = xdt_bf[:, p * 2 * SSD_HEAD_DIM:(p + 1) * 2 * SSD_HEAD_DIM]
        ys = []
        for j in (2 * p, 2 * p + 1):
            colb = cum_w[:, j * L:(j + 1) * L]
            rowb = cumt_ref[pl.ds(base + j, 1), :]
            decay = jnp.where(causal, jnp.exp(colb - rowb), 0.0)
            ys.append(_dot((cb * decay).astype(BF16), xpair))
        y_parts.append(jnp.where(first_half, ys[0], ys[1]))
    y_diag = jnp.concatenate(y_parts, axis=1)

    prev = state_ref[...]
    y_off = _dot(cm, prev.astype(BF16)) * jnp.exp(cum_x)
    xw = (xdt * jnp.exp(last_x - cum_x)).astype(BF16)
    state_ref[...] = prev * jnp.exp(last_x) + _dot_tn(bm, xw)

    y = y_diag + y_off + xs * dsk_ref[...]
    yg = y * _silu(z_ref[...])
    ms = jnp.mean(yg * yg, axis=-1, keepdims=True)
    o_ref[...] = (yg * lax.rsqrt(ms + RMS_EPS) * nw_ref[...]).astype(o_ref.dtype)


def _ssd_mixer(proj, small, conv_w, conv_b, dtb_pad, alog_pad, dskip_x, norm_w):
    s = proj.shape[0]
    L, gw, n = SSD_CHUNK, SSD_GROUP_WIDTH, SSD_STATE
    xb, bb, cbk = SSD_WIDTH // gw, SSD_WIDTH // n, (SSD_WIDTH + SSD_GROUPS * n) // n
    nbytes = (2 * L * (2 * gw + 2 * n + LANES) * 4 + 2 * L * gw * 2
              + (L + CONV_PAD) * (gw + 2 * n) * 4 + LANES * L * 4 + n * gw * 4)
    return pl.pallas_call(
        _ssd_kernel,
        grid=(SSD_GROUPS, s // L),
        in_specs=[
            pl.BlockSpec((L, gw), lambda g, c: (c, OFF_X // gw + g)),
            pl.BlockSpec((L, n), lambda g, c: (c, OFF_B // n + g)),
            pl.BlockSpec((L, n), lambda g, c: (c, OFF_C // n + g)),
            pl.BlockSpec((L, gw), lambda g, c: (c, OFF_Z // gw + g)),
            pl.BlockSpec((L, LANES), lambda g, c: (c, 0)),
            pl.BlockSpec((SSD_CONV, gw), lambda g, c: (0, g)),
            pl.BlockSpec((SSD_CONV, n), lambda g, c: (0, bb + g)),
            pl.BlockSpec((SSD_CONV, n), lambda g, c: (0, cbk + g)),
            pl.BlockSpec((1, gw), lambda g, c: (0, g)),
            pl.BlockSpec((1, n), lambda g, c: (0, bb + g)),
            pl.BlockSpec((1, n), lambda g, c: (0, cbk + g)),
            pl.BlockSpec((1, LANES), lambda g, c: (0, 0)),
            pl.BlockSpec((1, LANES), lambda g, c: (0, 0)),
            pl.BlockSpec((1, gw), lambda g, c: (0, g)),
            pl.BlockSpec((1, gw), lambda g, c: (0, g)),
        ],
        out_specs=pl.BlockSpec((L, gw), lambda g, c: (c, g)),
        out_shape=jax.ShapeDtypeStruct((s, SSD_WIDTH), BF16),
        scratch_shapes=[
            pltpu.VMEM((L + CONV_PAD, gw), F32),
            pltpu.VMEM((L + CONV_PAD, n), F32),
            pltpu.VMEM((L + CONV_PAD, n), F32),
            pltpu.VMEM((LANES, L), F32),
            pltpu.VMEM((n, gw), F32),
        ],
        compiler_params=pltpu.CompilerParams(
            dimension_semantics=("arbitrary", "arbitrary"),
            vmem_limit_bytes=_vmem_limit(nbytes)),
        name="ssd_mixer",
    )(proj, proj, proj, proj, small,
      conv_w, conv_w, conv_w, conv_b, conv_b, conv_b,
      dtb_pad, alog_pad, dskip_x, norm_w.reshape(1, -1))


def kernel(x, norm_mix_w, w_in, gla_gate_w2, gla_gate_b, gla_norm_w, conv_w, conv_b,
           dt_bias, a_log, d_skip, ssd_norm_w, w_out, norm_mlp_w, w_up, w_down, norm_f_w):
    bsz, seq, d = x.shape
    h = x.reshape(bsz * seq, d)
    m = h.shape[0]
    depth = w_in.shape[0]
    gate_lo = N_GLA
    z_lo = gate_lo + GLA_GATE_RANK
    dt_lo = z_lo + SSD_WIDTH + SSD_XBC
    pad_heads = lambda v: jnp.pad(v, (SMALL_DT_OFF, 0)).reshape(1, LANES)

    for l in range(depth):
        w_in_t = w_in[l].T
        w_small_t = jnp.concatenate(
            [w_in_t[gate_lo:z_lo],
             jnp.zeros((SMALL_DT_OFF - GLA_GATE_RANK, d), w_in.dtype),
             w_in_t[dt_lo:dt_lo + SSD_HEADS]], axis=0)
        w2_pad = jnp.pad(gla_gate_w2[l], ((0, LANES - GLA_GATE_RANK), (0, 0))).astype(BF16)
        dskip_x = jnp.repeat(d_skip[l], SSD_HEAD_DIM).reshape(1, SSD_WIDTH)

        u = _rmsnorm(h, norm_mix_w[l], BF16)
        proj_gla = _proj(u, w_in_t, row0=0, n_out=N_GLA, tm=PROJ_TM, tn=PROJ_TN,
                         out_dtype=F32, name="in_proj_gla")
        proj_ssd = _proj(u, w_in_t, row0=z_lo, n_out=N_SSD, tm=PROJ_TM, tn=PROJ_TN,
                         out_dtype=F32, name="in_proj_ssd")
        small = _matmul_small(u, w_small_t, tm=1024, out_dtype=F32, name="in_proj_small")

        o_gla = _gla_mixer(proj_gla, small, w2_pad, gla_gate_b[l], gla_norm_w[l])
        o_ssd = _ssd_mixer(proj_ssd, small, conv_w[l], conv_b[l].reshape(1, -1),
                           pad_heads(dt_bias[l]), pad_heads(a_log[l]), dskip_x, ssd_norm_w[l])

        h = _matmul_res([o_gla, o_ssd], w_out[l].astype(BF16), h,
                        tm=1024, tn=1024, tk=2048, name="out_proj")
        u = _rmsnorm(h, norm_mlp_w[l], BF16)
        hid = _proj(u, w_up[l], n_out=w_up.shape[2], tm=PROJ_TM, tn=PROJ_TN,
                    out_dtype=BF16, relu_sq=True, name="mlp_up")
        h = _matmul_res([hid], w_down[l].astype(BF16), h,
                        tm=1024, tn=1024, tk=2048, name="mlp_down")

    out = _rmsnorm(h, norm_f_w, x.dtype)
    return out.reshape(bsz, seq, d)
```

```python
import functools

import jax
import jax.numpy as jnp
from jax import lax
from jax.experimental import pallas as pl
from jax.experimental.pallas import tpu as pltpu

F32 = jnp.float32
BF16 = jnp.bfloat16

D_MODEL = 4096
GLA_HEADS = 8
GLA_DK = 256
GLA_DV = 512
GLA_QK = GLA_HEADS * GLA_DK
GLA_WIDTH = GLA_HEADS * GLA_DV
GLA_GATE_RANK = 16
GLA_GATE_NORMALIZER = 16.0
GLA_CHUNK = 64
SSD_WIDTH = 4096
SSD_HEAD_DIM = 64
SSD_HEADS = SSD_WIDTH // SSD_HEAD_DIM
SSD_STATE = 128
SSD_GROUPS = 8
SSD_HEADS_PER_GROUP = SSD_HEADS // SSD_GROUPS
SSD_GROUP_WIDTH = SSD_HEADS_PER_GROUP * SSD_HEAD_DIM
SSD_CONV = 4
SSD_CHUNK = 128
SSD_XBC = SSD_WIDTH + 2 * SSD_GROUPS * SSD_STATE
RMS_EPS = 1e-5

LANES = 128
SUBLANES = 8
VMEM_BYTES_V7X = 64 * 1024 * 1024

OFF_Q = 0
OFF_K = OFF_Q + GLA_QK
OFF_V = OFF_K + GLA_QK
OFF_G = OFF_V + GLA_WIDTH
N_GLA = OFF_G + GLA_WIDTH
OFF_Z = 0
OFF_X = OFF_Z + SSD_WIDTH
OFF_B = OFF_X + SSD_WIDTH
OFF_C = OFF_B + SSD_GROUPS * SSD_STATE
N_SSD = OFF_C + SSD_GROUPS * SSD_STATE
SMALL_DT_OFF = LANES - SSD_HEADS


def _vmem_limit(nbytes):
    return int(min(nbytes + (8 << 20), VMEM_BYTES_V7X - (4 << 20)))


def _dot(a, b):
    return jnp.dot(a, b, preferred_element_type=F32)


def _dot_nt(a, b):
    return lax.dot_general(a, b, (((1,), (1,)), ((), ())), preferred_element_type=F32)


def _dot_tn(a, b):
    return lax.dot_general(a, b, (((0,), (0,)), ((), ())), preferred_element_type=F32)


def _split3(x):
    hi = x.astype(BF16)
    r1 = x - hi.astype(F32)
    mid = r1.astype(BF16)
    lo = (r1 - mid.astype(F32)).astype(BF16)
    return hi, mid, lo


def _select_dot(sel_bf16, x):
    hi, mid, lo = _split3(x)
    return _dot(sel_bf16, hi) + _dot(sel_bf16, mid) + _dot(sel_bf16, lo)


def _softplus(x):
    return jnp.maximum(x, 0.0) + jnp.log1p(jnp.exp(-jnp.abs(x)))


def _log_sigmoid(x):
    return jnp.minimum(x, 0.0) - jnp.log1p(jnp.exp(-jnp.abs(x)))


def _silu(x):
    return x * jax.nn.sigmoid(x)


def _rmsnorm_kernel(x_ref, w_ref, o_ref):
    x = x_ref[...]
    ms = jnp.mean(x * x, axis=-1, keepdims=True)
    o_ref[...] = (x * lax.rsqrt(ms + RMS_EPS) * w_ref[...]).astype(o_ref.dtype)


def _rmsnorm(x, w, out_dtype, tm=256):
    m, d = x.shape
    nbytes = 2 * tm * d * (4 + jnp.dtype(out_dtype).itemsize) + 2 * d * 4
    return pl.pallas_call(
        _rmsnorm_kernel,
        grid=(m // tm,),
        in_specs=[pl.BlockSpec((tm, d), lambda i: (i, 0)),
                  pl.BlockSpec((1, d), lambda i: (0, 0))],
        out_specs=pl.BlockSpec((tm, d), lambda i: (i, 0)),
        out_shape=jax.ShapeDtypeStruct((m, d), out_dtype),
        compiler_params=pltpu.CompilerParams(
            dimension_semantics=("arbitrary",), vmem_limit_bytes=_vmem_limit(nbytes)),
        name="rmsnorm",
    )(x, w.reshape(1, d))


PROJ_TM = 512
PROJ_TN = 1024


def _proj_kernel(a_ref, w_ref, o_ref, *, w_is_transposed, relu_sq):
    w = w_ref[...].astype(BF16)
    acc = _dot_nt(a_ref[...], w) if w_is_transposed else _dot(a_ref[...], w)
    if relu_sq:
        acc = jnp.square(jnp.maximum(acc, 0.0))
    o_ref[...] = acc.astype(o_ref.dtype)


def _proj(a, w, *, n_out, tm, tn, out_dtype, row0=None, relu_sq=False, name):
    m, kdim = a.shape
    assert n_out % tn == 0 and m % tm == 0
    if row0 is None:
        w_spec = pl.BlockSpec((kdim, tn), lambda j, i: (0, j))
    else:
        assert row0 % SUBLANES == 0
        w_spec = pl.BlockSpec((pl.Element(tn), pl.Element(kdim)),
                              lambda j, i: (pl.multiple_of(row0 + j * tn, SUBLANES), 0))
    nbytes = (2 * tm * kdim * 2 + 2 * kdim * tn * 4 + kdim * tn * 2
              + 2 * tm * tn * jnp.dtype(out_dtype).itemsize + tm * tn * 4)
    return pl.pallas_call(
        functools.partial(_proj_kernel, w_is_transposed=row0 is not None, relu_sq=relu_sq),
        grid=(n_out // tn, m // tm),
        in_specs=[pl.BlockSpec((tm, kdim), lambda j, i: (i, 0)), w_spec],
        out_specs=pl.BlockSpec((tm, tn), lambda j, i: (i, j)),
        out_shape=jax.ShapeDtypeStruct((m, n_out), out_dtype),
        compiler_params=pltpu.CompilerParams(
            dimension_semantics=("arbitrary", "arbitrary"),
            vmem_limit_bytes=_vmem_limit(nbytes)),
        name=name,
    )(a, w)


def _matmul_small(a, b_t, *, tm, out_dtype, name):
    m, kdim = a.shape
    n = b_t.shape[0]

    def body(a_ref, b_ref, o_ref):
        o_ref[...] = _dot_nt(a_ref[...], b_ref[...].astype(BF16)).astype(o_ref.dtype)

    nbytes = 2 * tm * kdim * 2 + 3 * kdim * n * 4 + 3 * tm * n * 4
    return pl.pallas_call(
        body,
        grid=(m // tm,),
        in_specs=[pl.BlockSpec((tm, kdim), lambda i: (i, 0)),
                  pl.BlockSpec((n, kdim), lambda i: (0, 0))],
        out_specs=pl.BlockSpec((tm, n), lambda i: (i, 0)),
        out_shape=jax.ShapeDtypeStruct((m, n), out_dtype),
        compiler_params=pltpu.CompilerParams(
            dimension_semantics=("arbitrary",), vmem_limit_bytes=_vmem_limit(nbytes)),
        name=name,
    )(a, b_t)


def _matmul_res_kernel(*refs, nk, n_lhs):
    a_refs = refs[:n_lhs]
    b_ref, res_ref, o_ref, acc_ref = refs[n_lhs:]
    k = pl.program_id(2)

    @pl.when(k == 0)
    def _():
        acc_ref[...] = jnp.zeros_like(acc_ref)

    if n_lhs == 1:
        a = a_refs[0][...]
    else:
        a = jnp.where(k < nk // 2, a_refs[0][...], a_refs[1][...])
    acc_ref[...] += _dot(a, b_ref[...])

    @pl.when(k == nk - 1)
    def _():
        o_ref[...] = (acc_ref[...] + res_ref[...]).astype(o_ref.dtype)


def _matmul_res(lhs, b, residual, *, tm, tn, tk, name):
    n_lhs = len(lhs)
    m = lhs[0].shape[0]
    kdim, n = b.shape
    nk = kdim // tk
    half = nk // 2
    if n_lhs == 1:
        in_specs = [pl.BlockSpec((tm, tk), lambda j, i, k: (i, k))]
    else:
        in_specs = [pl.BlockSpec((tm, tk), lambda j, i, k: (i, jnp.minimum(k, half - 1))),
                    pl.BlockSpec((tm, tk), lambda j, i, k: (i, jnp.maximum(k - half, 0)))]
    in_specs += [pl.BlockSpec((tk, tn), lambda j, i, k: (k, j)),
                 pl.BlockSpec((tm, tn), lambda j, i, k: (i, j))]
    nbytes = 2 * (n_lhs * tm * tk + tk * tn) * 2 + 4 * tm * tn * 4 + 2 * tm * tn * 4
    return pl.pallas_call(
        functools.partial(_matmul_res_kernel, nk=nk, n_lhs=n_lhs),
        grid=(n // tn, m // tm, nk),
        in_specs=in_specs,
        out_specs=pl.BlockSpec((tm, tn), lambda j, i, k: (i, j)),
        out_shape=jax.ShapeDtypeStruct((m, n), residual.dtype),
        scratch_shapes=[pltpu.VMEM((tm, tn), F32)],
        compiler_params=pltpu.CompilerParams(
            dimension_semantics=("arbitrary", "arbitrary", "arbitrary"),
            vmem_limit_bytes=_vmem_limit(nbytes)),
        name=name,
    )(*lhs, b, residual)


GLA_BLOCK = 256
GLA_HEADS_PER_STEP = 8


def _gla_kernel(q_ref, k_ref, v_ref, g_ref, sm_ref, w2_ref, gb_ref, nw_ref,
                o_ref, *state_refs):
    c = GLA_CHUNK
    dk, dv = GLA_DK, GLA_DV
    heads = range(GLA_HEADS_PER_STEP)
    kc = [slice(h * dk, (h + 1) * dk) for h in heads]
    vc = [slice(h * dv, (h + 1) * dv) for h in heads]

    @pl.when(pl.program_id(1) == 0)
    def _():
        for st in state_refs:
            st[...] = jnp.zeros_like(st)

    row = lax.broadcasted_iota(jnp.int32, (c, c), 0)
    col = lax.broadcasted_iota(jnp.int32, (c, c), 1)
    causal = row >= col
    tril = jnp.where(causal, 1.0, 0.0).astype(BF16)

    pre = _dot(sm_ref[...].astype(BF16), w2_ref[...]) + gb_ref[...]
    log_a = _log_sigmoid(pre) * (1.0 / GLA_GATE_NORMALIZER)

    for n in range(GLA_BLOCK // c):
        rows = slice(n * c, (n + 1) * c)
        b = [_select_dot(tril, log_a[rows, kc[h]]) for h in heads]
        b_last = [b[h][c - 1:c, :] for h in heads]
        qn = [q_ref[rows, kc[h]] * (GLA_DK ** -0.5) for h in heads]
        kn = [k_ref[rows, kc[h]] for h in heads]
        vn = [v_ref[rows, vc[h]].astype(BF16) for h in heads]
        q_dec = [(qn[h] * jnp.exp(b[h])).astype(BF16) for h in heads]
        k_dec = [(kn[h] * jnp.exp(-b[h])).astype(BF16) for h in heads]
        k_end = [(kn[h] * jnp.exp(b_last[h] - b[h])).astype(BF16) for h in heads]
        scores = [jnp.where(causal, _dot_nt(q_dec[h], k_dec[h]), 0.0) for h in heads]
        state = [state_refs[h][...] for h in heads]
        o = [_dot(scores[h].astype(BF16), vn[h]) + _dot_nt(q_dec[h], state[h].astype(BF16))
             for h in heads]
        for h in heads:
            state_refs[h][...] = state[h] * jnp.exp(b_last[h]) + _dot_tn(vn[h], k_end[h])
        ms = [jnp.mean(o[h] * o[h], axis=-1, keepdims=True) for h in heads]
        o = [o[h] * lax.rsqrt(ms[h] + RMS_EPS) * nw_ref[...] for h in heads]
        for h in heads:
            o_ref[rows, vc[h]] = (o[h] * _silu(g_ref[rows, vc[h]])).astype(o_ref.dtype)


def _gla_mixer(proj, small, w2_pad, gate_b, norm_w):
    s = proj.shape[0]
    t = GLA_BLOCK
    hp = GLA_HEADS_PER_STEP
    dkb, dvb = hp * GLA_DK, hp * GLA_DV
    nbytes = (2 * t * (2 * dkb + 2 * dvb + LANES) * 4 + 2 * t * dvb * 2
              + hp * GLA_DV * GLA_DK * 4 + 4 * t * dkb * 4)
    return pl.pallas_call(
        _gla_kernel,
        grid=(GLA_HEADS // hp, s // t),
        in_specs=[
            pl.BlockSpec((t, dkb), lambda h, i: (i, OFF_Q // dkb + h)),
            pl.BlockSpec((t, dkb), lambda h, i: (i, OFF_K // dkb + h)),
            pl.BlockSpec((t, dvb), lambda h, i: (i, OFF_V // dvb + h)),
            pl.BlockSpec((t, dvb), lambda h, i: (i, OFF_G // dvb + h)),
            pl.BlockSpec((t, LANES), lambda h, i: (i, 0)),
            pl.BlockSpec((LANES, dkb), lambda h, i: (0, h)),
            pl.BlockSpec((1, dkb), lambda h, i: (0, h)),
            pl.BlockSpec((1, GLA_DV), lambda h, i: (0, 0)),
        ],
        out_specs=pl.BlockSpec((t, dvb), lambda h, i: (i, h)),
        out_shape=jax.ShapeDtypeStruct((s, GLA_WIDTH), BF16),
        scratch_shapes=[pltpu.VMEM((GLA_DV, GLA_DK), F32) for _ in range(hp)],
        compiler_params=pltpu.CompilerParams(
            dimension_semantics=("arbitrary", "arbitrary"),
            vmem_limit_bytes=_vmem_limit(nbytes)),
        name="gla_mixer",
    )(proj, proj, proj, proj, small, w2_pad, gate_b.reshape(1, -1), norm_w.reshape(1, -1))


CONV_PAD = SUBLANES


def _ssd_kernel(x_ref, b_ref, c_ref, z_ref, sm_ref, cwx_ref, cwb_ref, cwc_ref,
                cbx_ref, cbb_ref, cbc_ref, dtb_ref, alog_ref, dsk_ref, nw_ref,
                o_ref, xpad_ref, bpad_ref, cpad_ref, cumt_ref, state_ref):
    L = SSD_CHUNK
    gw, n, hd = SSD_GROUP_WIDTH, SSD_STATE, SSD_HEAD_DIM
    groups = range(SSD_GROUPS)
    heads = range(SSD_HEADS_PER_GROUP)

    @pl.when(pl.program_id(0) == 0)
    def _():
        xpad_ref[0:CONV_PAD, :] = jnp.zeros((CONV_PAD, xpad_ref.shape[1]), F32)
        bpad_ref[0:CONV_PAD, :] = jnp.zeros((CONV_PAD, bpad_ref.shape[1]), F32)
        cpad_ref[0:CONV_PAD, :] = jnp.zeros((CONV_PAD, cpad_ref.shape[1]), F32)
        state_ref[...] = jnp.zeros_like(state_ref)

    def conv_silu(raw_ref, pad_ref, w_ref, bias_ref, cols):
        raw = raw_ref[:, cols]
        pad_ref[CONV_PAD:CONV_PAD + L, cols] = raw
        padded = pad_ref[:, cols]
        acc = bias_ref[:, cols] + w_ref[SSD_CONV - 1:SSD_CONV, cols] * raw
        for d in range(1, SSD_CONV):
            k = SSD_CONV - 1 - d
            shifted = pltpu.roll(padded, d, axis=0)[CONV_PAD:, :]
            acc = acc + w_ref[k:k + 1, cols] * shifted
        pad_ref[0:CONV_PAD, cols] = raw[L - CONV_PAD:, :]
        return _silu(acc)

    gcols = [slice(g * gw, (g + 1) * gw) for g in groups]
    ncols = [slice(g * n, (g + 1) * n) for g in groups]
    xs = [conv_silu(x_ref, xpad_ref, cwx_ref, cbx_ref, gcols[g]) for g in groups]
    bm = [conv_silu(b_ref, bpad_ref, cwb_ref, cbb_ref, ncols[g]).astype(BF16) for g in groups]
    cm = [conv_silu(c_ref, cpad_ref, cwc_ref, cbc_ref, ncols[g]).astype(BF16) for g in groups]

    dt = _softplus(sm_ref[...] + dtb_ref[...])
    a = dt * (-jnp.exp(alog_ref[...]))
    row = lax.broadcasted_iota(jnp.int32, (L, L), 0)
    col = lax.broadcasted_iota(jnp.int32, (L, L), 1)
    causal = row >= col
    tril = jnp.where(causal, 1.0, 0.0).astype(BF16)
    a_cum = _select_dot(tril, a)
    cumt_ref[...] = a_cum.T

    def lane_bcast(v, lane, width):
        return jnp.broadcast_to(v[:, lane:lane + 1], (L, width))

    def widen(v, g):
        base = SMALL_DT_OFF + g * SSD_HEADS_PER_GROUP
        return jnp.concatenate([lane_bcast(v, base + j, hd) for j in heads], axis=1)

    dt_x = [widen(dt, g) for g in groups]
    cum_x = [widen(a_cum, g) for g in groups]
    last_x = [cum_x[g][L - 1:L, :] for g in groups]
    xdt = [xs[g] * dt_x[g] for g in groups]
    xdt_bf = [xdt[g].astype(BF16) for g in groups]
    cb = [_dot_nt(cm[g], bm[g]) for g in groups]

    lane = lax.broadcasted_iota(jnp.int32, (L, 2 * hd), 1)
    first_half = lane < hd
    y_parts = [[] for _ in groups]
    for p in range(SSD_HEADS_PER_GROUP // 2):
        for g in groups:
            base = SMALL_DT_OFF + g * SSD_HEADS_PER_GROUP
            xpair = xdt_bf[g][:, p * 2 * hd:(p + 1) * 2 * hd]
            ys = []
            for j in (2 * p, 2 * p + 1):
                colb = lane_bcast(a_cum, base + j, L)
                rowb = cumt_ref[base + j:base + j + 1, :]
                decay = jnp.where(causal, jnp.exp(colb - rowb), 0.0)
                ys.append(_dot((cb[g] * decay).astype(BF16), xpair))
            y_parts[g].append(jnp.where(first_half, ys[0], ys[1]))
    y_diag = [jnp.concatenate(y_parts[g], axis=1) for g in groups]

    prev = [state_ref[:, gcols[g]] for g in groups]
    y_off = [_dot(cm[g], prev[g].astype(BF16)) * jnp.exp(cum_x[g]) for g in groups]
    xw = [(xdt[g] * jnp.exp(last_x[g] - cum_x[g])).astype(BF16) for g in groups]
    for g in groups:
        state_ref[:, gcols[g]] = prev[g] * jnp.exp(last_x[g]) + _dot_tn(bm[g], xw[g])

    y = [y_diag[g] + y_off[g] + xs[g] * dsk_ref[:, gcols[g]] for g in groups]
    yg = [y[g] * _silu(z_ref[:, gcols[g]]) for g in groups]
    ms = [jnp.mean(yg[g] * yg[g], axis=-1, keepdims=True) for g in groups]
    for g in groups:
        o_ref[:, gcols[g]] = (yg[g] * lax.rsqrt(ms[g] + RMS_EPS)
                              * nw_ref[:, gcols[g]]).astype(o_ref.dtype)


def _ssd_mixer(proj, small, conv_w, conv_b, dtb_pad, alog_pad, dskip_x, norm_w):
    s = proj.shape[0]
    L, w, gn = SSD_CHUNK, SSD_WIDTH, SSD_GROUPS * SSD_STATE
    full = lambda c: (0, 0)
    nbytes = (2 * L * (2 * w + 2 * gn + LANES) * 4 + 2 * L * w * 2
              + (L + CONV_PAD) * (w + 2 * gn) * 4 + LANES * L * 4 + SSD_STATE * w * 4
              + 8 * L * w * 4)
    return pl.pallas_call(
        _ssd_kernel,
        grid=(s // L,),
        in_specs=[
            pl.BlockSpec((L, w), lambda c: (c, OFF_X // w)),
            pl.BlockSpec((L, gn), lambda c: (c, OFF_B // gn)),
            pl.BlockSpec((L, gn), lambda c: (c, OFF_C // gn)),
            pl.BlockSpec((L, w), lambda c: (c, OFF_Z // w)),
            pl.BlockSpec((L, LANES), lambda c: (c, 0)),
            pl.BlockSpec((SSD_CONV, w), lambda c: (0, 0)),
            pl.BlockSpec((SSD_CONV, gn), lambda c: (0, w // gn)),
            pl.BlockSpec((SSD_CONV, gn), lambda c: (0, w // gn + 1)),
            pl.BlockSpec((1, w), lambda c: (0, 0)),
            pl.BlockSpec((1, gn), lambda c: (0, w // gn)),
            pl.BlockSpec((1, gn), lambda c: (0, w // gn + 1)),
            pl.BlockSpec((1, LANES), full),
            pl.BlockSpec((1, LANES), full),
            pl.BlockSpec((1, w), full),
            pl.BlockSpec((1, w), full),
        ],
        out_specs=pl.BlockSpec((L, w), lambda c: (c, 0)),
        out_shape=jax.ShapeDtypeStruct((s, w), BF16),
        scratch_shapes=[
            pltpu.VMEM((L + CONV_PAD, w), F32),
            pltpu.VMEM((L + CONV_PAD, gn), F32),
            pltpu.VMEM((L + CONV_PAD, gn), F32),
            pltpu.VMEM((LANES, L), F32),
            pltpu.VMEM((SSD_STATE, w), F32),
        ],
        compiler_params=pltpu.CompilerParams(
            dimension_semantics=("arbitrary",),
            vmem_limit_bytes=_vmem_limit(nbytes)),
        name="ssd_mixer",
    )(proj, proj, proj, proj, small,
      conv_w, conv_w, conv_w, conv_b, conv_b, conv_b,
      dtb_pad, alog_pad, dskip_x, norm_w.reshape(1, -1))


def kernel(x, norm_mix_w, w_in, gla_gate_w2, gla_gate_b, gla_norm_w, conv_w, conv_b,
           dt_bias, a_log, d_skip, ssd_norm_w, w_out, norm_mlp_w, w_up, w_down, norm_f_w):
    bsz, seq, d = x.shape
    h = x.reshape(bsz * seq, d)
    m = h.shape[0]
    depth = w_in.shape[0]
    gate_lo = N_GLA
    z_lo = gate_lo + GLA_GATE_RANK
    dt_lo = z_lo + SSD_WIDTH + SSD_XBC
    pad_heads = lambda v: jnp.pad(v, (SMALL_DT_OFF, 0)).reshape(1, LANES)

    for l in range(depth):
        w_in_t = w_in[l].T
        w_small_t = jnp.concatenate(
            [w_in_t[gate_lo:z_lo],
             jnp.zeros((SMALL_DT_OFF - GLA_GATE_RANK, d), w_in.dtype),
             w_in_t[dt_lo:dt_lo + SSD_HEADS]], axis=0)
        w2_pad = jnp.pad(gla_gate_w2[l], ((0, LANES - GLA_GATE_RANK), (0, 0))).astype(BF16)
        dskip_x = jnp.repeat(d_skip[l], SSD_HEAD_DIM).reshape(1, SSD_WIDTH)

        u = _rmsnorm(h, norm_mix_w[l], BF16)
        proj_gla = _proj(u, w_in_t, row0=0, n_out=N_GLA, tm=PROJ_TM, tn=PROJ_TN,
                         out_dtype=F32, name="in_proj_gla")
        proj_ssd = _proj(u, w_in_t, row0=z_lo, n_out=N_SSD, tm=PROJ_TM, tn=PROJ_TN,
                         out_dtype=F32, name="in_proj_ssd")
        small = _matmul_small(u, w_small_t, tm=1024, out_dtype=F32, name="in_proj_small")

        o_gla = _gla_mixer(proj_gla, small, w2_pad, gla_gate_b[l], gla_norm_w[l])
        o_ssd = _ssd_mixer(proj_ssd, small, conv_w[l], conv_b[l].reshape(1, -1),
                           pad_heads(dt_bias[l]), pad_heads(a_log[l]), dskip_x, ssd_norm_w[l])

        h = _matmul_res([o_gla, o_ssd], w_out[l].astype(BF16), h,
                        tm=1024, tn=1024, tk=2048, name="out_proj")
        u = _rmsnorm(h, norm_mlp_w[l], BF16)
        hid = _proj(u, w_up[l], n_out=w_up.shape[2], tm=PROJ_TM, tn=PROJ_TN,
                    out_dtype=BF16, relu_sq=True, name="mlp_up")
        h = _matmul_res([hid], w_down[l].astype(BF16), h,
                        tm=1024, tn=1024, tk=2048, name="mlp_down")

    out = _rmsnorm(h, norm_f_w, x.dtype)
    return out.reshape(bsz, seq, d)
```

```python
import functools

import jax
import jax.numpy as jnp
from jax import lax
from jax.experimental import pallas as pl
from jax.experimental.pallas import tpu as pltpu

F32 = jnp.float32
BF16 = jnp.bfloat16

D_MODEL = 4096
GLA_HEADS = 8
GLA_DK = 256
GLA_DV = 512
GLA_QK = GLA_HEADS * GLA_DK
GLA_WIDTH = GLA_HEADS * GLA_DV
GLA_GATE_RANK = 16
GLA_GATE_NORMALIZER = 16.0
GLA_CHUNK = 64
SSD_WIDTH = 4096
SSD_HEAD_DIM = 64
SSD_HEADS = SSD_WIDTH // SSD_HEAD_DIM
SSD_STATE = 128
SSD_GROUPS = 8
SSD_HEADS_PER_GROUP = SSD_HEADS // SSD_GROUPS
SSD_GROUP_WIDTH = SSD_HEADS_PER_GROUP * SSD_HEAD_DIM
SSD_CONV = 4
SSD_CHUNK = 128
SSD_XBC = SSD_WIDTH + 2 * SSD_GROUPS * SSD_STATE
RMS_EPS = 1e-5

LANES = 128
SUBLANES = 8
VMEM_BYTES_V7X = 64 * 1024 * 1024

OFF_Q = 0
OFF_K = OFF_Q + GLA_QK
OFF_V = OFF_K + GLA_QK
OFF_G = OFF_V + GLA_WIDTH
N_GLA = OFF_G + GLA_WIDTH
OFF_Z = 0
OFF_X = OFF_Z + SSD_WIDTH
OFF_B = OFF_X + SSD_WIDTH
OFF_C = OFF_B + SSD_GROUPS * SSD_STATE
N_SSD = OFF_C + SSD_GROUPS * SSD_STATE
SMALL_DT_OFF = LANES - SSD_HEADS


def _vmem_limit(nbytes):
    return int(min(nbytes + (8 << 20), VMEM_BYTES_V7X - (4 << 20)))


def _dot(a, b):
    return jnp.dot(a, b, preferred_element_type=F32)


def _dot_nt(a, b):
    return lax.dot_general(a, b, (((1,), (1,)), ((), ())), preferred_element_type=F32)


def _dot_tn(a, b):
    return lax.dot_general(a, b, (((0,), (0,)), ((), ())), preferred_element_type=F32)


def _split3(x):
    hi = x.astype(BF16)
    r1 = x - hi.astype(F32)
    mid = r1.astype(BF16)
    lo = (r1 - mid.astype(F32)).astype(BF16)
    return hi, mid, lo


def _select_dot(sel_bf16, x):
    hi, mid, lo = _split3(x)
    return _dot(sel_bf16, hi) + _dot(sel_bf16, mid) + _dot(sel_bf16, lo)


def _softplus(x):
    return jnp.maximum(x, 0.0) + jnp.log1p(jnp.exp(-jnp.abs(x)))


def _log_sigmoid(x):
    return jnp.minimum(x, 0.0) - jnp.log1p(jnp.exp(-jnp.abs(x)))


def _silu(x):
    return x * jax.nn.sigmoid(x)


def _rmsnorm_kernel(x_ref, w_ref, o_ref):
    x = x_ref[...]
    ms = jnp.mean(x * x, axis=-1, keepdims=True)
    o_ref[...] = (x * lax.rsqrt(ms + RMS_EPS) * w_ref[...]).astype(o_ref.dtype)


def _rmsnorm(x, w, out_dtype, tm=256):
    m, d = x.shape
    nbytes = 2 * tm * d * (4 + jnp.dtype(out_dtype).itemsize) + 2 * d * 4
    return pl.pallas_call(
        _rmsnorm_kernel,
        grid=(m // tm,),
        in_specs=[pl.BlockSpec((tm, d), lambda i: (i, 0)),
                  pl.BlockSpec((1, d), lambda i: (0, 0))],
        out_specs=pl.BlockSpec((tm, d), lambda i: (i, 0)),
        out_shape=jax.ShapeDtypeStruct((m, d), out_dtype),
        compiler_params=pltpu.CompilerParams(
            dimension_semantics=("arbitrary",), vmem_limit_bytes=_vmem_limit(nbytes)),
        name="rmsnorm",
    )(x, w.reshape(1, d))


PROJ_TM = 512
PROJ_TN = 1024


def _proj_kernel(a_ref, w_ref, o_ref, *, w_is_transposed, relu_sq):
    w = w_ref[...].astype(BF16)
    acc = _dot_nt(a_ref[...], w) if w_is_transposed else _dot(a_ref[...], w)
    if relu_sq:
        acc = jnp.square(jnp.maximum(acc, 0.0))
    o_ref[...] = acc.astype(o_ref.dtype)


def _proj(a, w, *, n_out, tm, tn, out_dtype, row0=None, relu_sq=False, name):
    m, kdim = a.shape
    assert n_out % tn == 0 and m % tm == 0
    if row0 is None:
        w_spec = pl.BlockSpec((kdim, tn), lambda j, i: (0, j))
    else:
        assert row0 % SUBLANES == 0
        w_spec = pl.BlockSpec((pl.Element(tn), pl.Element(kdim)),
                              lambda j, i: (pl.multiple_of(row0 + j * tn, SUBLANES), 0))
    nbytes = (2 * tm * kdim * 2 + 2 * kdim * tn * 4 + kdim * tn * 2
              + 2 * tm * tn * jnp.dtype(out_dtype).itemsize + tm * tn * 4)
    return pl.pallas_call(
        functools.partial(_proj_kernel, w_is_transposed=row0 is not None, relu_sq=relu_sq),
        grid=(n_out // tn, m // tm),
        in_specs=[pl.BlockSpec((tm, kdim), lambda j, i: (i, 0)), w_spec],
        out_specs=pl.BlockSpec((tm, tn), lambda j, i: (i, j)),
        out_shape=jax.ShapeDtypeStruct((m, n_out), out_dtype),
        compiler_params=pltpu.CompilerParams(
            dimension_semantics=("arbitrary", "arbitrary"),
            vmem_limit_bytes=_vmem_limit(nbytes)),
        name=name,
    )(a, w)


def _matmul_small(a, b_t, *, tm, out_dtype, name):
    m, kdim = a.shape
    n = b_t.shape[0]

    def body(a_ref, b_ref, o_ref):
        o_ref[...] = _dot_nt(a_ref[...], b_ref[...].astype(BF16)).astype(o_ref.dtype)

    nbytes = 2 * tm * kdim * 2 + 3 * kdim * n * 4 + 3 * tm * n * 4
    return pl.pallas_call(
        body,
        grid=(m // tm,),
        in_specs=[pl.BlockSpec((tm, kdim), lambda i: (i, 0)),
                  pl.BlockSpec((n, kdim), lambda i: (0, 0))],
        out_specs=pl.BlockSpec((tm, n), lambda i: (i, 0)),
        out_shape=jax.ShapeDtypeStruct((m, n), out_dtype),
        compiler_params=pltpu.CompilerParams(
            dimension_semantics=("arbitrary",), vmem_limit_bytes=_vmem_limit(nbytes)),
        name=name,
    )(a, b_t)


def _matmul_res_fullk_kernel(*refs, n_lhs):
    a_refs = refs[:n_lhs]
    b_ref, res_ref, o_ref = refs[n_lhs:]
    kpart = b_ref.shape[0] // n_lhs
    acc = _dot(a_refs[0][...], b_ref[0:kpart, :])
    for l in range(1, n_lhs):
        acc = acc + _dot(a_refs[l][...], b_ref[l * kpart:(l + 1) * kpart, :])
    o_ref[...] = (acc + res_ref[...]).astype(o_ref.dtype)


def _matmul_res_fullk(lhs, b, residual, *, tm, tn, name):
    n_lhs = len(lhs)
    m, kpart = lhs[0].shape
    kdim, n = b.shape
    nbytes = 2 * (n_lhs * tm * kpart + kdim * tn) * 2 + 4 * tm * tn * 4 + 2 * tm * tn * 4
    return pl.pallas_call(
        functools.partial(_matmul_res_fullk_kernel, n_lhs=n_lhs),
        grid=(n // tn, m // tm),
        in_specs=[pl.BlockSpec((tm, kpart), lambda j, i: (i, 0)) for _ in lhs]
        + [pl.BlockSpec((kdim, tn), lambda j, i: (0, j)),
           pl.BlockSpec((tm, tn), lambda j, i: (i, j))],
        out_specs=pl.BlockSpec((tm, tn), lambda j, i: (i, j)),
        out_shape=jax.ShapeDtypeStruct((m, n), residual.dtype),
        compiler_params=pltpu.CompilerParams(
            dimension_semantics=("arbitrary", "arbitrary"),
            vmem_limit_bytes=_vmem_limit(nbytes)),
        name=name,
    )(*lhs, b, residual)


GLA_BLOCK = 256
GLA_HEADS_PER_STEP = 8


def _gla_kernel(q_ref, k_ref, v_ref, g_ref, sm_ref, w2_ref, gb_ref, nw_ref,
                o_ref, *state_refs):
    c = GLA_CHUNK
    dk, dv = GLA_DK, GLA_DV
    heads = range(GLA_HEADS_PER_STEP)
    kc = [slice(h * dk, (h + 1) * dk) for h in heads]
    vc = [slice(h * dv, (h + 1) * dv) for h in heads]

    @pl.when(pl.program_id(1) == 0)
    def _():
        for st in state_refs:
            st[...] = jnp.zeros_like(st)

    row = lax.broadcasted_iota(jnp.int32, (c, c), 0)
    col = lax.broadcasted_iota(jnp.int32, (c, c), 1)
    causal = row >= col
    tril = jnp.where(causal, 1.0, 0.0).astype(BF16)

    pre = _dot(sm_ref[...].astype(BF16), w2_ref[...]) + gb_ref[...]
    log_a = _log_sigmoid(pre) * (1.0 / GLA_GATE_NORMALIZER)

    for n in range(GLA_BLOCK // c):
        rows = slice(n * c, (n + 1) * c)
        b = [_select_dot(tril, log_a[rows, kc[h]]) for h in heads]
        b_last = [b[h][c - 1:c, :] for h in heads]
        qn = [q_ref[rows, kc[h]] * (GLA_DK ** -0.5) for h in heads]
        kn = [k_ref[rows, kc[h]] for h in heads]
        vn = [v_ref[rows, vc[h]].astype(BF16) for h in heads]
        q_dec = [(qn[h] * jnp.exp(b[h])).astype(BF16) for h in heads]
        k_dec = [(kn[h] * jnp.exp(-b[h])).astype(BF16) for h in heads]
        k_end = [(kn[h] * jnp.exp(b_last[h] - b[h])).astype(BF16) for h in heads]
        scores = [jnp.where(causal, _dot_nt(q_dec[h], k_dec[h]), 0.0) for h in heads]
        state = [state_refs[h][...] for h in heads]
        o = [_dot(scores[h].astype(BF16), vn[h]) + _dot_nt(q_dec[h], state[h].astype(BF16))
             for h in heads]
        for h in heads:
            state_refs[h][...] = state[h] * jnp.exp(b_last[h]) + _dot_tn(vn[h], k_end[h])
        ms = [jnp.mean(o[h] * o[h], axis=-1, keepdims=True) for h in heads]
        o = [o[h] * lax.rsqrt(ms[h] + RMS_EPS) * nw_ref[...] for h in heads]
        for h in heads:
            o_ref[rows, vc[h]] = (o[h] * _silu(g_ref[rows, vc[h]])).astype(o_ref.dtype)


def _gla_mixer(proj, small, w2_pad, gate_b, norm_w):
    s = proj.shape[0]
    t = GLA_BLOCK
    hp = GLA_HEADS_PER_STEP
    dkb, dvb = hp * GLA_DK, hp * GLA_DV
    nbytes = (2 * t * (2 * dkb + 2 * dvb + LANES) * 4 + 2 * t * dvb * 2
              + hp * GLA_DV * GLA_DK * 4 + 4 * t * dkb * 4)
    return pl.pallas_call(
        _gla_kernel,
        grid=(GLA_HEADS // hp, s // t),
        in_specs=[
            pl.BlockSpec((t, dkb), lambda h, i: (i, OFF_Q // dkb + h)),
            pl.BlockSpec((t, dkb), lambda h, i: (i, OFF_K // dkb + h)),
            pl.BlockSpec((t, dvb), lambda h, i: (i, OFF_V // dvb + h)),
            pl.BlockSpec((t, dvb), lambda h, i: (i, OFF_G // dvb + h)),
            pl.BlockSpec((t, LANES), lambda h, i: (i, 0)),
            pl.BlockSpec((LANES, dkb), lambda h, i: (0, h)),
            pl.BlockSpec((1, dkb), lambda h, i: (0, h)),
            pl.BlockSpec((1, GLA_DV), lambda h, i: (0, 0)),
        ],
        out_specs=pl.BlockSpec((t, dvb), lambda h, i: (i, h)),
        out_shape=jax.ShapeDtypeStruct((s, GLA_WIDTH), BF16),
        scratch_shapes=[pltpu.VMEM((GLA_DV, GLA_DK), F32) for _ in range(hp)],
        compiler_params=pltpu.CompilerParams(
            dimension_semantics=("arbitrary", "arbitrary"),
            vmem_limit_bytes=_vmem_limit(nbytes)),
        name="gla_mixer",
    )(proj, proj, proj, proj, small, w2_pad, gate_b.reshape(1, -1), norm_w.reshape(1, -1))


CONV_PAD = SUBLANES


def _ssd_kernel(x_ref, b_ref, c_ref, z_ref, sm_ref, cwx_ref, cwb_ref, cwc_ref,
                cbx_ref, cbb_ref, cbc_ref, dtb_ref, alog_ref, dsk_ref, nw_ref,
                o_ref, xpad_ref, bpad_ref, cpad_ref, cumt_ref, state_ref):
    L = SSD_CHUNK
    gw, n, hd = SSD_GROUP_WIDTH, SSD_STATE, SSD_HEAD_DIM
    groups = range(SSD_GROUPS)
    heads = range(SSD_HEADS_PER_GROUP)

    @pl.when(pl.program_id(0) == 0)
    def _():
        xpad_ref[0:CONV_PAD, :] = jnp.zeros((CONV_PAD, xpad_ref.shape[1]), F32)
        bpad_ref[0:CONV_PAD, :] = jnp.zeros((CONV_PAD, bpad_ref.shape[1]), F32)
        cpad_ref[0:CONV_PAD, :] = jnp.zeros((CONV_PAD, cpad_ref.shape[1]), F32)
        state_ref[...] = jnp.zeros_like(state_ref)

    def conv_silu(raw_ref, pad_ref, w_ref, bias_ref, cols):
        raw = raw_ref[:, cols]
        pad_ref[CONV_PAD:CONV_PAD + L, cols] = raw
        padded = pad_ref[:, cols]
        acc = bias_ref[:, cols] + w_ref[SSD_CONV - 1:SSD_CONV, cols] * raw
        for d in range(1, SSD_CONV):
            k = SSD_CONV - 1 - d
            shifted = pltpu.roll(padded, d, axis=0)[CONV_PAD:, :]
            acc = acc + w_ref[k:k + 1, cols] * shifted
        pad_ref[0:CONV_PAD, cols] = raw[L - CONV_PAD:, :]
        return _silu(acc)

    gcols = [slice(g * gw, (g + 1) * gw) for g in groups]
    ncols = [slice(g * n, (g + 1) * n) for g in groups]
    xs = [conv_silu(x_ref, xpad_ref, cwx_ref, cbx_ref, gcols[g]) for g in groups]
    bm = [conv_silu(b_ref, bpad_ref, cwb_ref, cbb_ref, ncols[g]).astype(BF16) for g in groups]
    cm = [conv_silu(c_ref, cpad_ref, cwc_ref, cbc_ref, ncols[g]).astype(BF16) for g in groups]

    dt = _softplus(sm_ref[...] + dtb_ref[...])
    a = dt * (-jnp.exp(alog_ref[...]))
    row = lax.broadcasted_iota(jnp.int32, (L, L), 0)
    col = lax.broadcasted_iota(jnp.int32, (L, L), 1)
    causal = row >= col
    tril = jnp.where(causal, 1.0, 0.0).astype(BF16)
    a_cum = _select_dot(tril, a)
    cumt_ref[...] = a_cum.T

    def lane_bcast(v, lane, width):
        return jnp.broadcast_to(v[:, lane:lane + 1], (L, width))

    def widen(v, g):
        base = SMALL_DT_OFF + g * SSD_HEADS_PER_GROUP
        return jnp.concatenate([lane_bcast(v, base + j, hd) for j in heads], axis=1)

    dt_x = [widen(dt, g) for g in groups]
    cum_x = [widen(a_cum, g) for g in groups]
    last_x = [cum_x[g][L - 1:L, :] for g in groups]
    xdt = [xs[g] * dt_x[g] for g in groups]
    xdt_bf = [xdt[g].astype(BF16) for g in groups]
    cb = [_dot_nt(cm[g], bm[g]) for g in groups]

    lane = lax.broadcasted_iota(jnp.int32, (L, 2 * hd), 1)
    first_half = lane < hd
    y_parts = [[] for _ in groups]
    for p in range(SSD_HEADS_PER_GROUP // 2):
        for g in groups:
            base = SMALL_DT_OFF + g * SSD_HEADS_PER_GROUP
            xpair = xdt_bf[g][:, p * 2 * hd:(p + 1) * 2 * hd]
            ys = []
            for j in (2 * p, 2 * p + 1):
                colb = lane_bcast(a_cum, base + j, L)
                rowb = cumt_ref[base + j:base + j + 1, :]
                decay = jnp.where(causal, jnp.exp(colb - rowb), 0.0)
                ys.append(_dot((cb[g] * decay).astype(BF16), xpair))
            y_parts[g].append(jnp.where(first_half, ys[0], ys[1]))
    y_diag = [jnp.concatenate(y_parts[g], axis=1) for g in groups]

    prev = [state_ref[:, gcols[g]] for g in groups]
    y_off = [_dot(cm[g], prev[g].astype(BF16)) * jnp.exp(cum_x[g]) for g in groups]
    xw = [(xdt[g] * jnp.exp(last_x[g] - cum_x[g])).astype(BF16) for g in groups]
    for g in groups:
        state_ref[:, gcols[g]] = prev[g] * jnp.exp(last_x[g]) + _dot_tn(bm[g], xw[g])

    y = [y_diag[g] + y_off[g] + xs[g] * dsk_ref[:, gcols[g]] for g in groups]
    yg = [y[g] * _silu(z_ref[:, gcols[g]]) for g in groups]
    ms = [jnp.mean(yg[g] * yg[g], axis=-1, keepdims=True) for g in groups]
    for g in groups:
        o_ref[:, gcols[g]] = (yg[g] * lax.rsqrt(ms[g] + RMS_EPS)
                              * nw_ref[:, gcols[g]]).astype(o_ref.dtype)


def _ssd_mixer(proj, small, conv_w, conv_b, dtb_pad, alog_pad, dskip_x, norm_w):
    s = proj.shape[0]
    L, w, gn = SSD_CHUNK, SSD_WIDTH, SSD_GROUPS * SSD_STATE
    full = lambda c: (0, 0)
    nbytes = (2 * L * (2 * w + 2 * gn + LANES) * 4 + 2 * L * w * 2
              + (L + CONV_PAD) * (w + 2 * gn) * 4 + LANES * L * 4 + SSD_STATE * w * 4
              + 8 * L * w * 4)
    return pl.pallas_call(
        _ssd_kernel,
        grid=(s // L,),
        in_specs=[
            pl.BlockSpec((L, w), lambda c: (c, OFF_X // w)),
            pl.BlockSpec((L, gn), lambda c: (c, OFF_B // gn)),
            pl.BlockSpec((L, gn), lambda c: (c, OFF_C // gn)),
            pl.BlockSpec((L, w), lambda c: (c, OFF_Z // w)),
            pl.BlockSpec((L, LANES), lambda c: (c, 0)),
            pl.BlockSpec((SSD_CONV, w), lambda c: (0, 0)),
            pl.BlockSpec((SSD_CONV, gn), lambda c: (0, w // gn)),
            pl.BlockSpec((SSD_CONV, gn), lambda c: (0, w // gn + 1)),
            pl.BlockSpec((1, w), lambda c: (0, 0)),
            pl.BlockSpec((1, gn), lambda c: (0, w // gn)),
            pl.BlockSpec((1, gn), lambda c: (0, w // gn + 1)),
            pl.BlockSpec((1, LANES), full),
            pl.BlockSpec((1, LANES), full),
            pl.BlockSpec((1, w), full),
            pl.BlockSpec((1, w), full),
        ],
        out_specs=pl.BlockSpec((L, w), lambda c: (c, 0)),
        out_shape=jax.ShapeDtypeStruct((s, w), BF16),
        scratch_shapes=[
            pltpu.VMEM((L + CONV_PAD, w), F32),
            pltpu.VMEM((L + CONV_PAD, gn), F32),
            pltpu.VMEM((L + CONV_PAD, gn), F32),
            pltpu.VMEM((LANES, L), F32),
            pltpu.VMEM((SSD_STATE, w), F32),
        ],
        compiler_params=pltpu.CompilerParams(
            dimension_semantics=("arbitrary",),
            vmem_limit_bytes=_vmem_limit(nbytes)),
        name="ssd_mixer",
    )(proj, proj, proj, proj, small,
      conv_w, conv_w, conv_w, conv_b, conv_b, conv_b,
      dtb_pad, alog_pad, dskip_x, norm_w.reshape(1, -1))


def kernel(x, norm_mix_w, w_in, gla_gate_w2, gla_gate_b, gla_norm_w, conv_w, conv_b,
           dt_bias, a_log, d_skip, ssd_norm_w, w_out, norm_mlp_w, w_up, w_down, norm_f_w):
    bsz, seq, d = x.shape
    h = x.reshape(bsz * seq, d)
    m = h.shape[0]
    depth = w_in.shape[0]
    gate_lo = N_GLA
    z_lo = gate_lo + GLA_GATE_RANK
    dt_lo = z_lo + SSD_WIDTH + SSD_XBC
    pad_heads = lambda v: jnp.pad(v, (SMALL_DT_OFF, 0)).reshape(1, LANES)

    for l in range(depth):
        w_in_t = w_in[l].T
        w_small_t = jnp.concatenate(
            [w_in_t[gate_lo:z_lo],
             jnp.zeros((SMALL_DT_OFF - GLA_GATE_RANK, d), w_in.dtype),
             w_in_t[dt_lo:dt_lo + SSD_HEADS]], axis=0)
        w2_pad = jnp.pad(gla_gate_w2[l], ((0, LANES - GLA_GATE_RANK), (0, 0))).astype(BF16)
        dskip_x = jnp.repeat(d_skip[l], SSD_HEAD_DIM).reshape(1, SSD_WIDTH)

        u = _rmsnorm(h, norm_mix_w[l], BF16)
        proj_gla = _proj(u, w_in_t, row0=0, n_out=N_GLA, tm=PROJ_TM, tn=PROJ_TN,
                         out_dtype=F32, name="in_proj_gla")
        proj_ssd = _proj(u, w_in_t, row0=z_lo, n_out=N_SSD, tm=PROJ_TM, tn=PROJ_TN,
                         out_dtype=F32, name="in_proj_ssd")
        small = _matmul_small(u, w_small_t, tm=1024, out_dtype=F32, name="in_proj_small")

        o_gla = _gla_mixer(proj_gla, small, w2_pad, gla_gate_b[l], gla_norm_w[l])
        o_ssd = _ssd_mixer(proj_ssd, small, conv_w[l], conv_b[l].reshape(1, -1),
                           pad_heads(dt_bias[l]), pad_heads(a_log[l]), dskip_x, ssd_norm_w[l])

        h = _matmul_res_fullk([o_gla, o_ssd], w_out[l].astype(BF16), h,
                              tm=512, tn=512, name="out_proj")
        u = _rmsnorm(h, norm_mlp_w[l], BF16)
        hid = _proj(u, w_up[l], n_out=w_up.shape[2], tm=PROJ_TM, tn=PROJ_TN,
                    out_dtype=BF16, relu_sq=True, name="mlp_up")
        h = _matmul_res_fullk([hid], w_down[l].astype(BF16), h,
                              tm=256, tn=512, name="mlp_down")

    out = _rmsnorm(h, norm_f_w, x.dtype)
    return out.reshape(bsz, seq, d)
```

```python
import functools

import jax
import jax.numpy as jnp
from jax import lax
from jax.experimental import pallas as pl
from jax.experimental.pallas import tpu as pltpu

F32 = jnp.float32
BF16 = jnp.bfloat16

D_MODEL = 4096
GLA_HEADS = 8
GLA_DK = 256
GLA_DV = 512
GLA_QK = GLA_HEADS * GLA_DK
GLA_WIDTH = GLA_HEADS * GLA_DV
GLA_GATE_RANK = 16
GLA_GATE_NORMALIZER = 16.0
GLA_CHUNK = 64
SSD_WIDTH = 4096
SSD_HEAD_DIM = 64
SSD_HEADS = SSD_WIDTH // SSD_HEAD_DIM
SSD_STATE = 128
SSD_GROUPS = 8
SSD_HEADS_PER_GROUP = SSD_HEADS // SSD_GROUPS
SSD_GROUP_WIDTH = SSD_HEADS_PER_GROUP * SSD_HEAD_DIM
SSD_CONV = 4
SSD_CHUNK = 128
SSD_XBC = SSD_WIDTH + 2 * SSD_GROUPS * SSD_STATE
RMS_EPS = 1e-5

LANES = 128
SUBLANES = 8
VMEM_BYTES_V7X = 64 * 1024 * 1024

OFF_Q = 0
OFF_K = OFF_Q + GLA_QK
OFF_V = OFF_K + GLA_QK
OFF_G = OFF_V + GLA_WIDTH
N_GLA = OFF_G + GLA_WIDTH
OFF_Z = 0
OFF_X = OFF_Z + SSD_WIDTH
OFF_B = OFF_X + SSD_WIDTH
OFF_C = OFF_B + SSD_GROUPS * SSD_STATE
N_SSD = OFF_C + SSD_GROUPS * SSD_STATE
SMALL_DT_OFF = LANES - SSD_HEADS


def _vmem_limit(nbytes):
    return int(min(nbytes + (8 << 20), VMEM_BYTES_V7X - (4 << 20)))


def _dot(a, b):
    return jnp.dot(a, b, preferred_element_type=F32)


def _dot_nt(a, b):
    return lax.dot_general(a, b, (((1,), (1,)), ((), ())), preferred_element_type=F32)


def _dot_tn(a, b):
    return lax.dot_general(a, b, (((0,), (0,)), ((), ())), preferred_element_type=F32)


def _split3(x):
    hi = x.astype(BF16)
    r1 = x - hi.astype(F32)
    mid = r1.astype(BF16)
    lo = (r1 - mid.astype(F32)).astype(BF16)
    return hi, mid, lo


def _select_dot(sel_bf16, x):
    hi, mid, lo = _split3(x)
    return _dot(sel_bf16, hi) + _dot(sel_bf16, mid) + _dot(sel_bf16, lo)


def _softplus(x):
    return jnp.maximum(x, 0.0) + jnp.log1p(jnp.exp(-jnp.abs(x)))


def _log_sigmoid(x):
    return jnp.minimum(x, 0.0) - jnp.log1p(jnp.exp(-jnp.abs(x)))


def _silu(x):
    return x * jax.nn.sigmoid(x)


def _rmsnorm_kernel(x_ref, w_ref, *rest):
    if len(rest) == 1:
        (o_ref,) = rest
    else:
        narrow_ref, o_ref, small_ref = rest
    x = x_ref[...]
    ms = jnp.mean(x * x, axis=-1, keepdims=True)
    y = (x * lax.rsqrt(ms + RMS_EPS) * w_ref[...]).astype(o_ref.dtype)
    o_ref[...] = y
    if len(rest) > 1:
        small_ref[...] = _dot_nt(y, narrow_ref[...].astype(BF16))


def _rmsnorm(x, w, out_dtype, tm=256, narrow_t=None):
    m, d = x.shape
    nbytes = 2 * tm * d * (4 + jnp.dtype(out_dtype).itemsize) + 2 * d * 4
    in_specs = [pl.BlockSpec((tm, d), lambda i: (i, 0)),
                pl.BlockSpec((1, d), lambda i: (0, 0))]
    out_specs = pl.BlockSpec((tm, d), lambda i: (i, 0))
    out_shape = jax.ShapeDtypeStruct((m, d), out_dtype)
    args = [x, w.reshape(1, d)]
    if narrow_t is not None:
        n = narrow_t.shape[0]
        in_specs.append(pl.BlockSpec((n, d), lambda i: (0, 0)))
        args.append(narrow_t)
        out_specs = [out_specs, pl.BlockSpec((tm, n), lambda i: (i, 0))]
        out_shape = [out_shape, jax.ShapeDtypeStruct((m, n), F32)]
        nbytes += 3 * n * d * 4 + 2 * tm * n * 4
    return pl.pallas_call(
        _rmsnorm_kernel,
        grid=(m // tm,),
        in_specs=in_specs,
        out_specs=out_specs,
        out_shape=out_shape,
        compiler_params=pltpu.CompilerParams(
            dimension_semantics=("arbitrary",), vmem_limit_bytes=_vmem_limit(nbytes)),
        name="rmsnorm",
    )(*args)


SIDE_BLOCKS = 128
PROJ_TM = 512
PROJ_TN = 1024


def _side_cast_specs(side, n_steps, ni):
    rows, cols = side.shape
    blk = rows // SIDE_BLOCKS
    assert rows % SIDE_BLOCKS == 0 and blk % (2 * SUBLANES) == 0 and n_steps >= SIDE_BLOCKS
    index = lambda j, i: (jnp.minimum(j * ni + i, SIDE_BLOCKS - 1), 0)
    spec = pl.BlockSpec((blk, cols), index)
    return spec, spec, jax.ShapeDtypeStruct((rows, cols), BF16), 2 * blk * cols * (4 + 2)


def _proj_kernel(*refs, w_is_transposed, relu_sq, has_side):
    if has_side:
        a_ref, w_ref, side_ref, o_ref, side_o_ref = refs
        side_o_ref[...] = side_ref[...].astype(BF16)
    else:
        a_ref, w_ref, o_ref = refs
    w = w_ref[...].astype(BF16)
    acc = _dot_nt(a_ref[...], w) if w_is_transposed else _dot(a_ref[...], w)
    if relu_sq:
        acc = jnp.square(jnp.maximum(acc, 0.0))
    o_ref[...] = acc.astype(o_ref.dtype)


def _proj(a, w, *, n_out, tm, tn, out_dtype, row0=None, relu_sq=False, side=None, name):
    m, kdim = a.shape
    assert n_out % tn == 0 and m % tm == 0
    nj, ni = n_out // tn, m // tm
    if row0 is None:
        w_spec = pl.BlockSpec((kdim, tn), lambda j, i: (0, j))
    else:
        assert row0 % SUBLANES == 0
        w_spec = pl.BlockSpec((pl.Element(tn), pl.Element(kdim)),
                              lambda j, i: (pl.multiple_of(row0 + j * tn, SUBLANES), 0))
    wbytes = w.dtype.itemsize
    nbytes = (2 * tm * kdim * 2 + 2 * kdim * tn * wbytes + kdim * tn * 2 * (wbytes > 2)
              + 2 * tm * tn * jnp.dtype(out_dtype).itemsize + tm * tn * 4)
    in_specs = [pl.BlockSpec((tm, kdim), lambda j, i: (i, 0)), w_spec]
    out_specs = pl.BlockSpec((tm, tn), lambda j, i: (i, j))
    out_shape = jax.ShapeDtypeStruct((m, n_out), out_dtype)
    args = [a, w]
    if side is not None:
        s_in, s_out, s_shape, s_bytes = _side_cast_specs(side, nj * ni, ni)
        in_specs.append(s_in)
        args.append(side)
        out_specs, out_shape = [out_specs, s_out], [out_shape, s_shape]
        nbytes += s_bytes
    return pl.pallas_call(
        functools.partial(_proj_kernel, w_is_transposed=row0 is not None, relu_sq=relu_sq,
                          has_side=side is not None),
        grid=(nj, ni),
        in_specs=in_specs,
        out_specs=out_specs,
        out_shape=out_shape,
        compiler_params=pltpu.CompilerParams(
            dimension_semantics=("arbitrary", "arbitrary"),
            vmem_limit_bytes=_vmem_limit(nbytes)),
        name=name,
    )(*args)


def _matmul_res_fullk_kernel(*refs, n_lhs, has_side):
    a_refs = refs[:n_lhs]
    if has_side:
        b_ref, res_ref, side_ref, o_ref, side_o_ref = refs[n_lhs:]
        side_o_ref[...] = side_ref[...].astype(BF16)
    else:
        b_ref, res_ref, o_ref = refs[n_lhs:]
    kpart = b_ref.shape[0] // n_lhs
    acc = _dot(a_refs[0][...], b_ref[0:kpart, :])
    for l in range(1, n_lhs):
        acc = acc + _dot(a_refs[l][...], b_ref[l * kpart:(l + 1) * kpart, :])
    o_ref[...] = (acc + res_ref[...]).astype(o_ref.dtype)


def _matmul_res_fullk(lhs, b, residual, *, tm, tn, side=None, name):
    n_lhs = len(lhs)
    m, kpart = lhs[0].shape
    kdim, n = b.shape
    nj, ni = n // tn, m // tm
    nbytes = 2 * (n_lhs * tm * kpart + kdim * tn) * 2 + 4 * tm * tn * 4 + 2 * tm * tn * 4
    in_specs = ([pl.BlockSpec((tm, kpart), lambda j, i: (i, 0)) for _ in lhs]
                + [pl.BlockSpec((kdim, tn), lambda j, i: (0, j)),
                   pl.BlockSpec((tm, tn), lambda j, i: (i, j))])
    out_specs = pl.BlockSpec((tm, tn), lambda j, i: (i, j))
    out_shape = jax.ShapeDtypeStruct((m, n), residual.dtype)
    args = [*lhs, b, residual]
    if side is not None:
        s_in, s_out, s_shape, s_bytes = _side_cast_specs(side, nj * ni, ni)
        in_specs.append(s_in)
        args.append(side)
        out_specs, out_shape = [out_specs, s_out], [out_shape, s_shape]
        nbytes += s_bytes
    return pl.pallas_call(
        functools.partial(_matmul_res_fullk_kernel, n_lhs=n_lhs, has_side=side is not None),
        grid=(nj, ni),
        in_specs=in_specs,
        out_specs=out_specs,
        out_shape=out_shape,
        compiler_params=pltpu.CompilerParams(
            dimension_semantics=("arbitrary", "arbitrary"),
            vmem_limit_bytes=_vmem_limit(nbytes)),
        name=name,
    )(*args)


GLA_BLOCK = 256
GLA_HEADS_PER_STEP = 8


def _gla_kernel(q_ref, k_ref, v_ref, g_ref, sm_ref, w2_ref, gb_ref, nw_ref,
                o_ref, *state_refs):
    c = GLA_CHUNK
    dk, dv = GLA_DK, GLA_DV
    heads = range(GLA_HEADS_PER_STEP)
    kc = [slice(h * dk, (h + 1) * dk) for h in heads]
    vc = [slice(h * dv, (h + 1) * dv) for h in heads]

    @pl.when(pl.program_id(1) == 0)
    def _():
        for st in state_refs:
            st[...] = jnp.zeros_like(st)

    row = lax.broadcasted_iota(jnp.int32, (c, c), 0)
    col = lax.broadcasted_iota(jnp.int32, (c, c), 1)
    causal = row >= col
    tril = jnp.where(causal, 1.0, 0.0).astype(BF16)

    pre = _dot(sm_ref[...].astype(BF16), w2_ref[...]) + gb_ref[...]
    log_a = _log_sigmoid(pre) * (1.0 / GLA_GATE_NORMALIZER)

    for n in range(GLA_BLOCK // c):
        rows = slice(n * c, (n + 1) * c)
        b = [_select_dot(tril, log_a[rows, kc[h]]) for h in heads]
        b_last = [b[h][c - 1:c, :] for h in heads]
        qn = [q_ref[rows, kc[h]] * (GLA_DK ** -0.5) for h in heads]
        kn = [k_ref[rows, kc[h]] for h in heads]
        vn = [v_ref[rows, vc[h]].astype(BF16) for h in heads]
        q_dec = [(qn[h] * jnp.exp(b[h])).astype(BF16) for h in heads]
        k_dec = [(kn[h] * jnp.exp(-b[h])).astype(BF16) for h in heads]
        k_end = [(kn[h] * jnp.exp(b_last[h] - b[h])).astype(BF16) for h in heads]
        scores = [jnp.where(causal, _dot_nt(q_dec[h], k_dec[h]), 0.0) for h in heads]
        state = [state_refs[h][...] for h in heads]
        o = [_dot(scores[h].astype(BF16), vn[h]) + _dot_nt(q_dec[h], state[h].astype(BF16))
             for h in heads]
        for h in heads:
            state_refs[h][...] = state[h] * jnp.exp(b_last[h]) + _dot_tn(vn[h], k_end[h])
        ms = [jnp.mean(o[h] * o[h], axis=-1, keepdims=True) for h in heads]
        o = [o[h] * lax.rsqrt(ms[h] + RMS_EPS) * nw_ref[...] for h in heads]
        for h in heads:
            o_ref[rows, vc[h]] = (o[h] * _silu(g_ref[rows, vc[h]])).astype(o_ref.dtype)


def _gla_mixer(proj, small, w2_pad, gate_b, norm_w):
    s = proj.shape[0]
    t = GLA_BLOCK
    hp = GLA_HEADS_PER_STEP
    dkb, dvb = hp * GLA_DK, hp * GLA_DV
    nbytes = (2 * t * (2 * dkb + 2 * dvb + LANES) * 4 + 2 * t * dvb * 2
              + hp * GLA_DV * GLA_DK * 4 + 4 * t * dkb * 4)
    return pl.pallas_call(
        _gla_kernel,
        grid=(GLA_HEADS // hp, s // t),
        in_specs=[
            pl.BlockSpec((t, dkb), lambda h, i: (i, OFF_Q // dkb + h)),
            pl.BlockSpec((t, dkb), lambda h, i: (i, OFF_K // dkb + h)),
            pl.BlockSpec((t, dvb), lambda h, i: (i, OFF_V // dvb + h)),
            pl.BlockSpec((t, dvb), lambda h, i: (i, OFF_G // dvb + h)),
            pl.BlockSpec((t, LANES), lambda h, i: (i, 0)),
            pl.BlockSpec((LANES, dkb), lambda h, i: (0, h)),
            pl.BlockSpec((1, dkb), lambda h, i: (0, h)),
            pl.BlockSpec((1, GLA_DV), lambda h, i: (0, 0)),
        ],
        out_specs=pl.BlockSpec((t, dvb), lambda h, i: (i, h)),
        out_shape=jax.ShapeDtypeStruct((s, GLA_WIDTH), BF16),
        scratch_shapes=[pltpu.VMEM((GLA_DV, GLA_DK), F32) for _ in range(hp)],
        compiler_params=pltpu.CompilerParams(
            dimension_semantics=("arbitrary", "arbitrary"),
            vmem_limit_bytes=_vmem_limit(nbytes)),
        name="gla_mixer",
    )(proj, proj, proj, proj, small, w2_pad, gate_b.reshape(1, -1), norm_w.reshape(1, -1))


CONV_PAD = SUBLANES


def _ssd_kernel(x_ref, b_ref, c_ref, z_ref, sm_ref, cwx_ref, cwb_ref, cwc_ref,
                cbx_ref, cbb_ref, cbc_ref, dtb_ref, alog_ref, dsk_ref, nw_ref,
                o_ref, xpad_ref, bpad_ref, cpad_ref, cumt_ref, state_ref):
    L = SSD_CHUNK
    gw, n, hd = SSD_GROUP_WIDTH, SSD_STATE, SSD_HEAD_DIM
    groups = range(SSD_GROUPS)
    heads = range(SSD_HEADS_PER_GROUP)

    @pl.when(pl.program_id(0) == 0)
    def _():
        xpad_ref[0:CONV_PAD, :] = jnp.zeros((CONV_PAD, xpad_ref.shape[1]), F32)
        bpad_ref[0:CONV_PAD, :] = jnp.zeros((CONV_PAD, bpad_ref.shape[1]), F32)
        cpad_ref[0:CONV_PAD, :] = jnp.zeros((CONV_PAD, cpad_ref.shape[1]), F32)
        state_ref[...] = jnp.zeros_like(state_ref)

    def conv_silu(raw_ref, pad_ref, w_ref, bias_ref, cols):
        raw = raw_ref[:, cols]
        pad_ref[CONV_PAD:CONV_PAD + L, cols] = raw
        padded = pad_ref[:, cols]
        acc = bias_ref[:, cols] + w_ref[SSD_CONV - 1:SSD_CONV, cols] * raw
        for d in range(1, SSD_CONV):
            k = SSD_CONV - 1 - d
            shifted = pltpu.roll(padded, d, axis=0)[CONV_PAD:, :]
            acc = acc + w_ref[k:k + 1, cols] * shifted
        pad_ref[0:CONV_PAD, cols] = raw[L - CONV_PAD:, :]
        return _silu(acc)

    gcols = [slice(g * gw, (g + 1) * gw) for g in groups]
    ncols = [slice(g * n, (g + 1) * n) for g in groups]
    xs = [conv_silu(x_ref, xpad_ref, cwx_ref, cbx_ref, gcols[g]) for g in groups]
    bm = [conv_silu(b_ref, bpad_ref, cwb_ref, cbb_ref, ncols[g]).astype(BF16) for g in groups]
    cm = [conv_silu(c_ref, cpad_ref, cwc_ref, cbc_ref, ncols[g]).astype(BF16) for g in groups]

    dt = _softplus(sm_ref[...] + dtb_ref[...])
    a = dt * (-jnp.exp(alog_ref[...]))
    row = lax.broadcasted_iota(jnp.int32, (L, L), 0)
    col = lax.broadcasted_iota(jnp.int32, (L, L), 1)
    causal = row >= col
    tril = jnp.where(causal, 1.0, 0.0).astype(BF16)
    a_cum = _select_dot(tril, a)
    cumt_ref[...] = a_cum.T

    def lane_bcast(v, lane, width):
        return jnp.broadcast_to(v[:, lane:lane + 1], (L, width))

    def widen(v, g):
        base = SMALL_DT_OFF + g * SSD_HEADS_PER_GROUP
        return jnp.concatenate([lane_bcast(v, base + j, hd) for j in heads], axis=1)

    dt_x = [widen(dt, g) for g in groups]
    cum_x = [widen(a_cum, g) for g in groups]
    last_x = [cum_x[g][L - 1:L, :] for g in groups]
    xdt = [xs[g] * dt_x[g] for g in groups]
    xdt_bf = [xdt[g].astype(BF16) for g in groups]
    cb = [_dot_nt(cm[g], bm[g]) for g in groups]

    lane = lax.broadcasted_iota(jnp.int32, (L, 2 * hd), 1)
    first_half = lane < hd
    y_parts = [[] for _ in groups]
    for p in range(SSD_HEADS_PER_GROUP // 2):
        for g in groups:
            base = SMALL_DT_OFF + g * SSD_HEADS_PER_GROUP
            xpair = xdt_bf[g][:, p * 2 * hd:(p + 1) * 2 * hd]
            ys = []
            for j in (2 * p, 2 * p + 1):
                colb = lane_bcast(a_cum, base + j, L)
                rowb = cumt_ref[base + j:base + j + 1, :]
                decay = jnp.where(causal, jnp.exp(colb - rowb), 0.0)
                ys.append(_dot((cb[g] * decay).astype(BF16), xpair))
            y_parts[g].append(jnp.where(first_half, ys[0], ys[1]))
    y_diag = [jnp.concatenate(y_parts[g], axis=1) for g in groups]

    prev = [state_ref[:, gcols[g]] for g in groups]
    y_off = [_dot(cm[g], prev[g].astype(BF16)) * jnp.exp(cum_x[g]) for g in groups]
    xw = [(xdt[g] * jnp.exp(last_x[g] - cum_x[g])).astype(BF16) for g in groups]
    for g in groups:
        state_ref[:, gcols[g]] = prev[g] * jnp.exp(last_x[g]) + _dot_tn(bm[g], xw[g])

    y = [y_diag[g] + y_off[g] + xs[g] * dsk_ref[:, gcols[g]] for g in groups]
    yg = [y[g] * _silu(z_ref[:, gcols[g]]) for g in groups]
    ms = [jnp.mean(yg[g] * yg[g], axis=-1, keepdims=True) for g in groups]
    for g in groups:
        o_ref[:, gcols[g]] = (yg[g] * lax.rsqrt(ms[g] + RMS_EPS)
                              * nw_ref[:, gcols[g]]).astype(o_ref.dtype)


def _ssd_mixer(proj, small, conv_w, conv_b, dtb_pad, alog_pad, dskip_x, norm_w):
    s = proj.shape[0]
    L, w, gn = SSD_CHUNK, SSD_WIDTH, SSD_GROUPS * SSD_STATE
    full = lambda c: (0, 0)
    nbytes = (2 * L * (2 * w + 2 * gn + LANES) * 4 + 2 * L * w * 2
              + (L + CONV_PAD) * (w + 2 * gn) * 4 + LANES * L * 4 + SSD_STATE * w * 4
              + 8 * L * w * 4)
    return pl.pallas_call(
        _ssd_kernel,
        grid=(s // L,),
        in_specs=[
            pl.BlockSpec((L, w), lambda c: (c, OFF_X // w)),
            pl.BlockSpec((L, gn), lambda c: (c, OFF_B // gn)),
            pl.BlockSpec((L, gn), lambda c: (c, OFF_C // gn)),
            pl.BlockSpec((L, w), lambda c: (c, OFF_Z // w)),
            pl.BlockSpec((L, LANES), lambda c: (c, 0)),
            pl.BlockSpec((SSD_CONV, w), lambda c: (0, 0)),
            pl.BlockSpec((SSD_CONV, gn), lambda c: (0, w // gn)),
            pl.BlockSpec((SSD_CONV, gn), lambda c: (0, w // gn + 1)),
            pl.BlockSpec((1, w), lambda c: (0, 0)),
            pl.BlockSpec((1, gn), lambda c: (0, w // gn)),
            pl.BlockSpec((1, gn), lambda c: (0, w // gn + 1)),
            pl.BlockSpec((1, LANES), full),
            pl.BlockSpec((1, LANES), full),
            pl.BlockSpec((1, w), full),
            pl.BlockSpec((1, w), full),
        ],
        out_specs=pl.BlockSpec((L, w), lambda c: (c, 0)),
        out_shape=jax.ShapeDtypeStruct((s, w), BF16),
        scratch_shapes=[
            pltpu.VMEM((L + CONV_PAD, w), F32),
            pltpu.VMEM((L + CONV_PAD, gn), F32),
            pltpu.VMEM((L + CONV_PAD, gn), F32),
            pltpu.VMEM((LANES, L), F32),
            pltpu.VMEM((SSD_STATE, w), F32),
        ],
        compiler_params=pltpu.CompilerParams(
            dimension_semantics=("arbitrary",),
            vmem_limit_bytes=_vmem_limit(nbytes)),
        name="ssd_mixer",
    )(proj, proj, proj, proj, small,
      conv_w, conv_w, conv_w, conv_b, conv_b, conv_b,
      dtb_pad, alog_pad, dskip_x, norm_w.reshape(1, -1))


def kernel(x, norm_mix_w, w_in, gla_gate_w2, gla_gate_b, gla_norm_w, conv_w, conv_b,
           dt_bias, a_log, d_skip, ssd_norm_w, w_out, norm_mlp_w, w_up, w_down, norm_f_w):
    bsz, seq, d = x.shape
    h = x.reshape(bsz * seq, d)
    depth = w_in.shape[0]
    gate_lo = N_GLA
    z_lo = gate_lo + GLA_GATE_RANK
    dt_lo = z_lo + SSD_WIDTH + SSD_XBC
    pad_heads = lambda v: jnp.pad(v, (SMALL_DT_OFF, 0)).reshape(1, LANES)

    for l in range(depth):
        w_in_t = w_in[l].T
        w_small_t = jnp.concatenate(
            [w_in_t[gate_lo:z_lo],
             jnp.zeros((SMALL_DT_OFF - GLA_GATE_RANK, d), w_in.dtype),
             w_in_t[dt_lo:dt_lo + SSD_HEADS]], axis=0)
        w2_pad = jnp.pad(gla_gate_w2[l], ((0, LANES - GLA_GATE_RANK), (0, 0))).astype(BF16)
        dskip_x = jnp.repeat(d_skip[l], SSD_HEAD_DIM).reshape(1, SSD_WIDTH)

        u, small = _rmsnorm(h, norm_mix_w[l], BF16, narrow_t=w_small_t)
        proj_gla, w_out_bf = _proj(u, w_in_t, row0=0, n_out=N_GLA, tm=PROJ_TM, tn=PROJ_TN,
                                   out_dtype=F32, side=w_out[l], name="in_proj_gla")
        proj_ssd, w_up_bf = _proj(u, w_in_t, row0=z_lo, n_out=N_SSD, tm=PROJ_TM, tn=PROJ_TN,
                                  out_dtype=F32, side=w_up[l], name="in_proj_ssd")

        o_gla = _gla_mixer(proj_gla, small, w2_pad, gla_gate_b[l], gla_norm_w[l])
        o_ssd = _ssd_mixer(proj_ssd, small, conv_w[l], conv_b[l].reshape(1, -1),
                           pad_heads(dt_bias[l]), pad_heads(a_log[l]), dskip_x, ssd_norm_w[l])

        h, w_down_bf = _matmul_res_fullk([o_gla, o_ssd], w_out_bf, h, tm=512, tn=512,
                                         side=w_down[l], name="out_proj")
        u = _rmsnorm(h, norm_mlp_w[l], BF16)
        hid = _proj(u, w_up_bf, n_out=w_up.shape[2], tm=1024, tn=1024,
                    out_dtype=BF16, relu_sq=True, name="mlp_up")
        h = _matmul_res_fullk([hid], w_down_bf, h, tm=256, tn=512, name="mlp_down")

    out = _rmsnorm(h, norm_f_w, x.dtype)
    return out.reshape(bsz, seq, d)
```

```python
import functools

import jax
import jax.numpy as jnp
from jax import lax
from jax.experimental import pallas as pl
from jax.experimental.pallas import tpu as pltpu

F32 = jnp.float32
BF16 = jnp.bfloat16

D_MODEL = 4096
GLA_HEADS = 8
GLA_DK = 256
GLA_DV = 512
GLA_QK = GLA_HEADS * GLA_DK
GLA_WIDTH = GLA_HEADS * GLA_DV
GLA_GATE_RANK = 16
GLA_GATE_NORMALIZER = 16.0
GLA_CHUNK = 64
SSD_WIDTH = 4096
SSD_HEAD_DIM = 64
SSD_HEADS = SSD_WIDTH // SSD_HEAD_DIM
SSD_STATE = 128
SSD_GROUPS = 8
SSD_HEADS_PER_GROUP = SSD_HEADS // SSD_GROUPS
SSD_GROUP_WIDTH = SSD_HEADS_PER_GROUP * SSD_HEAD_DIM
SSD_CONV = 4
SSD_CHUNK = 128
SSD_XBC = SSD_WIDTH + 2 * SSD_GROUPS * SSD_STATE
RMS_EPS = 1e-5

LANES = 128
SUBLANES = 8
VMEM_BYTES_V7X = 64 * 1024 * 1024

OFF_Q = 0
OFF_K = OFF_Q + GLA_QK
OFF_V = OFF_K + GLA_QK
OFF_G = OFF_V + GLA_WIDTH
N_GLA = OFF_G + GLA_WIDTH
OFF_Z = 0
OFF_X = OFF_Z + SSD_WIDTH
OFF_B = OFF_X + SSD_WIDTH
OFF_C = OFF_B + SSD_GROUPS * SSD_STATE
N_SSD = OFF_C + SSD_GROUPS * SSD_STATE
SMALL_DT_OFF = LANES - SSD_HEADS


def _vmem_limit(nbytes):
    return int(min(nbytes + (8 << 20), VMEM_BYTES_V7X - (4 << 20)))


def _dot(a, b):
    return jnp.dot(a, b, preferred_element_type=F32)


def _dot_nt(a, b):
    return lax.dot_general(a, b, (((1,), (1,)), ((), ())), preferred_element_type=F32)


def _dot_tn(a, b):
    return lax.dot_general(a, b, (((0,), (0,)), ((), ())), preferred_element_type=F32)


def _split3(x):
    hi = x.astype(BF16)
    r1 = x - hi.astype(F32)
    mid = r1.astype(BF16)
    lo = (r1 - mid.astype(F32)).astype(BF16)
    return hi, mid, lo


def _select_dot(sel_bf16, x):
    hi, mid, lo = _split3(x)
    return _dot(sel_bf16, hi) + _dot(sel_bf16, mid) + _dot(sel_bf16, lo)


def _softplus(x):
    return jnp.maximum(x, 0.0) + jnp.log1p(jnp.exp(-jnp.abs(x)))


def _log_sigmoid(x):
    return jnp.minimum(x, 0.0) - jnp.log1p(jnp.exp(-jnp.abs(x)))


def _silu(x):
    return x * jax.nn.sigmoid(x)


def _rmsnorm_kernel(x_ref, w_ref, *rest):
    if len(rest) == 1:
        (o_ref,) = rest
    else:
        narrow_ref, o_ref, small_ref = rest
    x = x_ref[...]
    ms = jnp.mean(x * x, axis=-1, keepdims=True)
    y = (x * lax.rsqrt(ms + RMS_EPS) * w_ref[...]).astype(o_ref.dtype)
    o_ref[...] = y
    if len(rest) > 1:
        small_ref[...] = _dot_nt(y, narrow_ref[...].astype(BF16))


def _rmsnorm(x, w, out_dtype, tm=512, narrow_t=None):
    m, d = x.shape
    nbytes = 2 * tm * d * (4 + jnp.dtype(out_dtype).itemsize) + 2 * d * 4
    in_specs = [pl.BlockSpec((tm, d), lambda i: (i, 0)),
                pl.BlockSpec((1, d), lambda i: (0, 0))]
    out_specs = pl.BlockSpec((tm, d), lambda i: (i, 0))
    out_shape = jax.ShapeDtypeStruct((m, d), out_dtype)
    args = [x, w.reshape(1, d)]
    if narrow_t is not None:
        n = narrow_t.shape[0]
        in_specs.append(pl.BlockSpec((n, d), lambda i: (0, 0)))
        args.append(narrow_t)
        out_specs = [out_specs, pl.BlockSpec((tm, n), lambda i: (i, 0))]
        out_shape = [out_shape, jax.ShapeDtypeStruct((m, n), F32)]
        nbytes += 3 * n * d * 4 + 2 * tm * n * 4
    return pl.pallas_call(
        _rmsnorm_kernel,
        grid=(m // tm,),
        in_specs=in_specs,
        out_specs=out_specs,
        out_shape=out_shape,
        compiler_params=pltpu.CompilerParams(
            dimension_semantics=("arbitrary",), vmem_limit_bytes=_vmem_limit(nbytes)),
        name="rmsnorm",
    )(*args)


SIDE_BLOCKS = 128
PROJ_TM = 512
PROJ_TN = 1024


def _side_cast_specs(side, n_steps, ni):
    rows, cols = side.shape
    blk = rows // SIDE_BLOCKS
    assert rows % SIDE_BLOCKS == 0 and blk % (2 * SUBLANES) == 0 and n_steps >= SIDE_BLOCKS
    index = lambda j, i: (jnp.minimum(j * ni + i, SIDE_BLOCKS - 1), 0)
    spec = pl.BlockSpec((blk, cols), index)
    return spec, spec, jax.ShapeDtypeStruct((rows, cols), BF16), 2 * blk * cols * (4 + 2)


def _proj_kernel(*refs, w_is_transposed, relu_sq, has_side):
    if has_side:
        a_ref, w_ref, side_ref, o_ref, side_o_ref = refs
        side_o_ref[...] = side_ref[...].astype(BF16)
    else:
        a_ref, w_ref, o_ref = refs
    w = w_ref[...].astype(BF16)
    acc = _dot_nt(a_ref[...], w) if w_is_transposed else _dot(a_ref[...], w)
    if relu_sq:
        acc = jnp.square(jnp.maximum(acc, 0.0))
    o_ref[...] = acc.astype(o_ref.dtype)


def _proj(a, w, *, n_out, tm, tn, out_dtype, row0=None, relu_sq=False, side=None, name):
    m, kdim = a.shape
    assert n_out % tn == 0 and m % tm == 0
    nj, ni = n_out // tn, m // tm
    if row0 is None:
        w_spec = pl.BlockSpec((kdim, tn), lambda j, i: (0, j))
    else:
        assert row0 % SUBLANES == 0
        w_spec = pl.BlockSpec((pl.Element(tn), pl.Element(kdim)),
                              lambda j, i: (pl.multiple_of(row0 + j * tn, SUBLANES), 0))
    wbytes = w.dtype.itemsize
    nbytes = (2 * tm * kdim * 2 + 2 * kdim * tn * wbytes + kdim * tn * 2 * (wbytes > 2)
              + 2 * tm * tn * jnp.dtype(out_dtype).itemsize + tm * tn * 4)
    in_specs = [pl.BlockSpec((tm, kdim), lambda j, i: (i, 0)), w_spec]
    out_specs = pl.BlockSpec((tm, tn), lambda j, i: (i, j))
    out_shape = jax.ShapeDtypeStruct((m, n_out), out_dtype)
    args = [a, w]
    if side is not None:
        s_in, s_out, s_shape, s_bytes = _side_cast_specs(side, nj * ni, ni)
        in_specs.append(s_in)
        args.append(side)
        out_specs, out_shape = [out_specs, s_out], [out_shape, s_shape]
        nbytes += s_bytes
    return pl.pallas_call(
        functools.partial(_proj_kernel, w_is_transposed=row0 is not None, relu_sq=relu_sq,
                          has_side=side is not None),
        grid=(nj, ni),
        in_specs=in_specs,
        out_specs=out_specs,
        out_shape=out_shape,
        compiler_params=pltpu.CompilerParams(
            dimension_semantics=("arbitrary", "arbitrary"),
            vmem_limit_bytes=_vmem_limit(nbytes)),
        name=name,
    )(*args)


def _matmul_res_fullk_kernel(*refs, n_lhs, has_side):
    a_refs = refs[:n_lhs]
    if has_side:
        b_ref, res_ref, side_ref, o_ref, side_o_ref = refs[n_lhs:]
        side_o_ref[...] = side_ref[...].astype(BF16)
    else:
        b_ref, res_ref, o_ref = refs[n_lhs:]
    kpart = b_ref.shape[0] // n_lhs
    acc = _dot(a_refs[0][...], b_ref[0:kpart, :])
    for l in range(1, n_lhs):
        acc = acc + _dot(a_refs[l][...], b_ref[l * kpart:(l + 1) * kpart, :])
    o_ref[...] = (acc + res_ref[...]).astype(o_ref.dtype)


def _matmul_res_fullk(lhs, b, residual, *, tm, tn, side=None, name):
    n_lhs = len(lhs)
    m, kpart = lhs[0].shape
    kdim, n = b.shape
    nj, ni = n // tn, m // tm
    nbytes = 2 * (n_lhs * tm * kpart + kdim * tn) * 2 + 4 * tm * tn * 4 + 2 * tm * tn * 4
    in_specs = ([pl.BlockSpec((tm, kpart), lambda j, i: (i, 0)) for _ in lhs]
                + [pl.BlockSpec((kdim, tn), lambda j, i: (0, j)),
                   pl.BlockSpec((tm, tn), lambda j, i: (i, j))])
    out_specs = pl.BlockSpec((tm, tn), lambda j, i: (i, j))
    out_shape = jax.ShapeDtypeStruct((m, n), residual.dtype)
    args = [*lhs, b, residual]
    if side is not None:
        s_in, s_out, s_shape, s_bytes = _side_cast_specs(side, nj * ni, ni)
        in_specs.append(s_in)
        args.append(side)
        out_specs, out_shape = [out_specs, s_out], [out_shape, s_shape]
        nbytes += s_bytes
    return pl.pallas_call(
        functools.partial(_matmul_res_fullk_kernel, n_lhs=n_lhs, has_side=side is not None),
        grid=(nj, ni),
        in_specs=in_specs,
        out_specs=out_specs,
        out_shape=out_shape,
        compiler_params=pltpu.CompilerParams(
            dimension_semantics=("arbitrary", "arbitrary"),
            vmem_limit_bytes=_vmem_limit(nbytes)),
        name=name,
    )(*args)


GLA_BLOCK = 256
GLA_HEADS_PER_STEP = 8


def _gla_kernel(q_ref, k_ref, v_ref, g_ref, sm_ref, w2_ref, gb_ref, nw_ref,
                o_ref, *state_refs):
    c = GLA_CHUNK
    dk, dv = GLA_DK, GLA_DV
    heads = range(GLA_HEADS_PER_STEP)
    kc = [slice(h * dk, (h + 1) * dk) for h in heads]
    vc = [slice(h * dv, (h + 1) * dv) for h in heads]

    @pl.when(pl.program_id(1) == 0)
    def _():
        for st in state_refs:
            st[...] = jnp.zeros_like(st)

    row = lax.broadcasted_iota(jnp.int32, (c, c), 0)
    col = lax.broadcasted_iota(jnp.int32, (c, c), 1)
    causal = row >= col
    tril = jnp.where(causal, 1.0, 0.0).astype(BF16)

    pre = _dot(sm_ref[...].astype(BF16), w2_ref[...]) + gb_ref[...]
    log_a = _log_sigmoid(pre) * (1.0 / GLA_GATE_NORMALIZER)

    for n in range(GLA_BLOCK // c):
        rows = slice(n * c, (n + 1) * c)
        b = [_select_dot(tril, log_a[rows, kc[h]]) for h in heads]
        b_last = [b[h][c - 1:c, :] for h in heads]
        qn = [q_ref[rows, kc[h]] * (GLA_DK ** -0.5) for h in heads]
        kn = [k_ref[rows, kc[h]] for h in heads]
        vn = [v_ref[rows, vc[h]].astype(BF16) for h in heads]
        q_dec = [(qn[h] * jnp.exp(b[h])).astype(BF16) for h in heads]
        k_dec = [(kn[h] * jnp.exp(-b[h])).astype(BF16) for h in heads]
        k_end = [(kn[h] * jnp.exp(b_last[h] - b[h])).astype(BF16) for h in heads]
        scores = [jnp.where(causal, _dot_nt(q_dec[h], k_dec[h]), 0.0) for h in heads]
        state = [state_refs[h][...] for h in heads]
        o = [_dot(scores[h].astype(BF16), vn[h]) + _dot_nt(q_dec[h], state[h].astype(BF16))
             for h in heads]
        for h in heads:
            state_refs[h][...] = state[h] * jnp.exp(b_last[h]) + _dot_tn(vn[h], k_end[h])
        ms = [jnp.mean(o[h] * o[h], axis=-1, keepdims=True) for h in heads]
        o = [o[h] * lax.rsqrt(ms[h] + RMS_EPS) * nw_ref[...] for h in heads]
        for h in heads:
            o_ref[rows, vc[h]] = (o[h] * _silu(g_ref[rows, vc[h]])).astype(o_ref.dtype)


def _gla_mixer(proj, small, w2_pad, gate_b, norm_w):
    s = proj.shape[0]
    t = GLA_BLOCK
    hp = GLA_HEADS_PER_STEP
    dkb, dvb = hp * GLA_DK, hp * GLA_DV
    nbytes = (2 * t * (2 * dkb + 2 * dvb + LANES) * 4 + 2 * t * dvb * 2
              + hp * GLA_DV * GLA_DK * 4 + 4 * t * dkb * 4)
    return pl.pallas_call(
        _gla_kernel,
        grid=(GLA_HEADS // hp, s // t),
        in_specs=[
            pl.BlockSpec((t, dkb), lambda h, i: (i, OFF_Q // dkb + h)),
            pl.BlockSpec((t, dkb), lambda h, i: (i, OFF_K // dkb + h)),
            pl.BlockSpec((t, dvb), lambda h, i: (i, OFF_V // dvb + h)),
            pl.BlockSpec((t, dvb), lambda h, i: (i, OFF_G // dvb + h)),
            pl.BlockSpec((t, LANES), lambda h, i: (i, 0)),
            pl.BlockSpec((LANES, dkb), lambda h, i: (0, h)),
            pl.BlockSpec((1, dkb), lambda h, i: (0, h)),
            pl.BlockSpec((1, GLA_DV), lambda h, i: (0, 0)),
        ],
        out_specs=pl.BlockSpec((t, dvb), lambda h, i: (i, h)),
        out_shape=jax.ShapeDtypeStruct((s, GLA_WIDTH), BF16),
        scratch_shapes=[pltpu.VMEM((GLA_DV, GLA_DK), F32) for _ in range(hp)],
        compiler_params=pltpu.CompilerParams(
            dimension_semantics=("arbitrary", "arbitrary"),
            vmem_limit_bytes=_vmem_limit(nbytes)),
        name="gla_mixer",
    )(proj, proj, proj, proj, small, w2_pad, gate_b.reshape(1, -1), norm_w.reshape(1, -1))


CONV_PAD = SUBLANES


def _ssd_kernel(x_ref, b_ref, c_ref, z_ref, sm_ref, cwx_ref, cwb_ref, cwc_ref,
                cbx_ref, cbb_ref, cbc_ref, dtb_ref, alog_ref, dsk_ref, nw_ref,
                o_ref, xpad_ref, bpad_ref, cpad_ref, cumt_ref, state_ref):
    L = SSD_CHUNK
    gw, n, hd = SSD_GROUP_WIDTH, SSD_STATE, SSD_HEAD_DIM
    groups = range(SSD_GROUPS)
    heads = range(SSD_HEADS_PER_GROUP)

    @pl.when(pl.program_id(0) == 0)
    def _():
        xpad_ref[0:CONV_PAD, :] = jnp.zeros((CONV_PAD, xpad_ref.shape[1]), F32)
        bpad_ref[0:CONV_PAD, :] = jnp.zeros((CONV_PAD, bpad_ref.shape[1]), F32)
        cpad_ref[0:CONV_PAD, :] = jnp.zeros((CONV_PAD, cpad_ref.shape[1]), F32)
        state_ref[...] = jnp.zeros_like(state_ref)

    def conv_silu(raw_ref, pad_ref, w_ref, bias_ref, cols):
        raw = raw_ref[:, cols]
        pad_ref[CONV_PAD:CONV_PAD + L, cols] = raw
        padded = pad_ref[:, cols]
        acc = bias_ref[:, cols] + w_ref[SSD_CONV - 1:SSD_CONV, cols] * raw
        for d in range(1, SSD_CONV):
            k = SSD_CONV - 1 - d
            shifted = pltpu.roll(padded, d, axis=0)[CONV_PAD:, :]
            acc = acc + w_ref[k:k + 1, cols] * shifted
        pad_ref[0:CONV_PAD, cols] = raw[L - CONV_PAD:, :]
        return _silu(acc)

    gcols = [slice(g * gw, (g + 1) * gw) for g in groups]
    ncols = [slice(g * n, (g + 1) * n) for g in groups]
    xs = [conv_silu(x_ref, xpad_ref, cwx_ref, cbx_ref, gcols[g]) for g in groups]
    bm = [conv_silu(b_ref, bpad_ref, cwb_ref, cbb_ref, ncols[g]).astype(BF16) for g in groups]
    cm = [conv_silu(c_ref, cpad_ref, cwc_ref, cbc_ref, ncols[g]).astype(BF16) for g in groups]

    dt = _softplus(sm_ref[...] + dtb_ref[...])
    a = dt * (-jnp.exp(alog_ref[...]))
    row = lax.broadcasted_iota(jnp.int32, (L, L), 0)
    col = lax.broadcasted_iota(jnp.int32, (L, L), 1)
    causal = row >= col
    tril = jnp.where(causal, 1.0, 0.0).astype(BF16)
    a_cum = _select_dot(tril, a)
    cumt_ref[...] = a_cum.T

    def lane_bcast(v, lane, width):
        return jnp.broadcast_to(v[:, lane:lane + 1], (L, width))

    def widen(v, g):
        base = SMALL_DT_OFF + g * SSD_HEADS_PER_GROUP
        return jnp.concatenate([lane_bcast(v, base + j, hd) for j in heads], axis=1)

    dt_x = [widen(dt, g) for g in groups]
    cum_x = [widen(a_cum, g) for g in groups]
    last_x = [cum_x[g][L - 1:L, :] for g in groups]
    xdt = [xs[g] * dt_x[g] for g in groups]
    xdt_bf = [xdt[g].astype(BF16) for g in groups]
    cb = [_dot_nt(cm[g], bm[g]) for g in groups]

    lane = lax.broadcasted_iota(jnp.int32, (L, 2 * hd), 1)
    first_half = lane < hd
    y_parts = [[] for _ in groups]
    for p in range(SSD_HEADS_PER_GROUP // 2):
        for g in groups:
            base = SMALL_DT_OFF + g * SSD_HEADS_PER_GROUP
            xpair = xdt_bf[g][:, p * 2 * hd:(p + 1) * 2 * hd]
            ys = []
            for j in (2 * p, 2 * p + 1):
                colb = lane_bcast(a_cum, base + j, L)
                rowb = cumt_ref[base + j:base + j + 1, :]
                decay = jnp.where(causal, jnp.exp(colb - rowb), 0.0)
                ys.append(_dot((cb[g] * decay).astype(BF16), xpair))
            y_parts[g].append(jnp.where(first_half, ys[0], ys[1]))
    y_diag = [jnp.concatenate(y_parts[g], axis=1) for g in groups]

    prev = [state_ref[:, gcols[g]] for g in groups]
    y_off = [_dot(cm[g], prev[g].astype(BF16)) * jnp.exp(cum_x[g]) for g in groups]
    xw = [(xdt[g] * jnp.exp(last_x[g] - cum_x[g])).astype(BF16) for g in groups]
    for g in groups:
        state_ref[:, gcols[g]] = prev[g] * jnp.exp(last_x[g]) + _dot_tn(bm[g], xw[g])

    y = [y_diag[g] + y_off[g] + xs[g] * dsk_ref[:, gcols[g]] for g in groups]
    yg = [y[g] * _silu(z_ref[:, gcols[g]]) for g in groups]
    ms = [jnp.mean(yg[g] * yg[g], axis=-1, keepdims=True) for g in groups]
    for g in groups:
        o_ref[:, gcols[g]] = (yg[g] * lax.rsqrt(ms[g] + RMS_EPS)
                              * nw_ref[:, gcols[g]]).astype(o_ref.dtype)


def _ssd_mixer(proj, small, conv_w, conv_b, dtb_pad, alog_pad, dskip_x, norm_w):
    s = proj.shape[0]
    L, w, gn = SSD_CHUNK, SSD_WIDTH, SSD_GROUPS * SSD_STATE
    full = lambda c: (0, 0)
    nbytes = (2 * L * (2 * w + 2 * gn + LANES) * 4 + 2 * L * w * 2
              + (L + CONV_PAD) * (w + 2 * gn) * 4 + LANES * L * 4 + SSD_STATE * w * 4
              + 8 * L * w * 4)
    return pl.pallas_call(
        _ssd_kernel,
        grid=(s // L,),
        in_specs=[
            pl.BlockSpec((L, w), lambda c: (c, OFF_X // w)),
            pl.BlockSpec((L, gn), lambda c: (c, OFF_B // gn)),
            pl.BlockSpec((L, gn), lambda c: (c, OFF_C // gn)),
            pl.BlockSpec((L, w), lambda c: (c, OFF_Z // w)),
            pl.BlockSpec((L, LANES), lambda c: (c, 0)),
            pl.BlockSpec((SSD_CONV, w), lambda c: (0, 0)),
            pl.BlockSpec((SSD_CONV, gn), lambda c: (0, w // gn)),
            pl.BlockSpec((SSD_CONV, gn), lambda c: (0, w // gn + 1)),
            pl.BlockSpec((1, w), lambda c: (0, 0)),
            pl.BlockSpec((1, gn), lambda c: (0, w // gn)),
            pl.BlockSpec((1, gn), lambda c: (0, w // gn + 1)),
            pl.BlockSpec((1, LANES), full),
            pl.BlockSpec((1, LANES), full),
            pl.BlockSpec((1, w), full),
            pl.BlockSpec((1, w), full),
        ],
        out_specs=pl.BlockSpec((L, w), lambda c: (c, 0)),
        out_shape=jax.ShapeDtypeStruct((s, w), BF16),
        scratch_shapes=[
            pltpu.VMEM((L + CONV_PAD, w), F32),
            pltpu.VMEM((L + CONV_PAD, gn), F32),
            pltpu.VMEM((L + CONV_PAD, gn), F32),
            pltpu.VMEM((LANES, L), F32),
            pltpu.VMEM((SSD_STATE, w), F32),
        ],
        compiler_params=pltpu.CompilerParams(
            dimension_semantics=("arbitrary",),
            vmem_limit_bytes=_vmem_limit(nbytes)),
        name="ssd_mixer",
    )(proj, proj, proj, proj, small,
      conv_w, conv_w, conv_w, conv_b, conv_b, conv_b,
      dtb_pad, alog_pad, dskip_x, norm_w.reshape(1, -1))


def kernel(x, norm_mix_w, w_in, gla_gate_w2, gla_gate_b, gla_norm_w, conv_w, conv_b,
           dt_bias, a_log, d_skip, ssd_norm_w, w_out, norm_mlp_w, w_up, w_down, norm_f_w):
    bsz, seq, d = x.shape
    h = x.reshape(bsz * seq, d)
    depth = w_in.shape[0]
    gate_lo = N_GLA
    z_lo = gate_lo + GLA_GATE_RANK
    dt_lo = z_lo + SSD_WIDTH + SSD_XBC
    pad_heads = lambda v: jnp.pad(v, (SMALL_DT_OFF, 0)).reshape(1, LANES)

    for l in range(depth):
        w_in_t = w_in[l].T
        w_small_t = jnp.concatenate(
            [w_in_t[gate_lo:z_lo],
             jnp.zeros((SMALL_DT_OFF - GLA_GATE_RANK, d), w_in.dtype),
             w_in_t[dt_lo:dt_lo + SSD_HEADS]], axis=0)
        w2_pad = jnp.pad(gla_gate_w2[l], ((0, LANES - GLA_GATE_RANK), (0, 0))).astype(BF16)
        dskip_x = jnp.repeat(d_skip[l], SSD_HEAD_DIM).reshape(1, SSD_WIDTH)

        u, small = _rmsnorm(h, norm_mix_w[l], BF16, narrow_t=w_small_t)
        proj_gla, w_out_bf = _proj(u, w_in_t, row0=0, n_out=N_GLA, tm=PROJ_TM, tn=PROJ_TN,
                                   out_dtype=F32, side=w_out[l], name="in_proj_gla")
        proj_ssd, w_up_bf = _proj(u, w_in_t, row0=z_lo, n_out=N_SSD, tm=PROJ_TM, tn=PROJ_TN,
                                  out_dtype=F32, side=w_up[l], name="in_proj_ssd")

        o_gla = _gla_mixer(proj_gla, small, w2_pad, gla_gate_b[l], gla_norm_w[l])
        o_ssd = _ssd_mixer(proj_ssd, small, conv_w[l], conv_b[l].reshape(1, -1),
                           pad_heads(dt_bias[l]), pad_heads(a_log[l]), dskip_x, ssd_norm_w[l])

        h = _matmul_res_fullk([o_gla, o_ssd], w_out_bf, h, tm=512, tn=512, name="out_proj")
        u = _rmsnorm(h, norm_mlp_w[l], BF16)
        hid, w_down_bf = _proj(u, w_up_bf, n_out=w_up.shape[2], tm=1024, tn=1024,
                               out_dtype=BF16, relu_sq=True, side=w_down[l], name="mlp_up")
        h = _matmul_res_fullk([hid], w_down_bf, h, tm=256, tn=512, name="mlp_down")

    out = _rmsnorm(h, norm_f_w, x.dtype)
    return out.reshape(bsz, seq, d)
```

```python
import functools

import jax
import jax.numpy as jnp
from jax import lax
from jax.experimental import pallas as pl
from jax.experimental.pallas import tpu as pltpu

F32 = jnp.float32
BF16 = jnp.bfloat16

D_MODEL = 4096
GLA_HEADS = 8
GLA_DK = 256
GLA_DV = 512
GLA_QK = GLA_HEADS * GLA_DK
GLA_WIDTH = GLA_HEADS * GLA_DV
GLA_GATE_RANK = 16
GLA_GATE_NORMALIZER = 16.0
GLA_CHUNK = 64
SSD_WIDTH = 4096
SSD_HEAD_DIM = 64
SSD_HEADS = SSD_WIDTH // SSD_HEAD_DIM
SSD_STATE = 128
SSD_GROUPS = 8
SSD_HEADS_PER_GROUP = SSD_HEADS // SSD_GROUPS
SSD_GROUP_WIDTH = SSD_HEADS_PER_GROUP * SSD_HEAD_DIM
SSD_CONV = 4
SSD_CHUNK = 128
SSD_XBC = SSD_WIDTH + 2 * SSD_GROUPS * SSD_STATE
RMS_EPS = 1e-5

LANES = 128
SUBLANES = 8
VMEM_BYTES_V7X = 64 * 1024 * 1024

OFF_Q = 0
OFF_K = OFF_Q + GLA_QK
OFF_V = OFF_K + GLA_QK
OFF_G = OFF_V + GLA_WIDTH
N_GLA = OFF_G + GLA_WIDTH
OFF_Z = 0
OFF_X = OFF_Z + SSD_WIDTH
OFF_B = OFF_X + SSD_WIDTH
OFF_C = OFF_B + SSD_GROUPS * SSD_STATE
N_SSD = OFF_C + SSD_GROUPS * SSD_STATE
SMALL_DT_OFF = LANES - SSD_HEADS


def _vmem_limit(nbytes):
    return int(min(nbytes + (8 << 20), VMEM_BYTES_V7X - (4 << 20)))


def _dot(a, b):
    return jnp.dot(a, b, preferred_element_type=F32)


def _dot_nt(a, b):
    return lax.dot_general(a, b, (((1,), (1,)), ((), ())), preferred_element_type=F32)


def _dot_tn(a, b):
    return lax.dot_general(a, b, (((0,), (0,)), ((), ())), preferred_element_type=F32)


def _split3(x):
    hi = x.astype(BF16)
    r1 = x - hi.astype(F32)
    mid = r1.astype(BF16)
    lo = (r1 - mid.astype(F32)).astype(BF16)
    return hi, mid, lo


def _select_dot(sel_bf16, x):
    hi, mid, lo = _split3(x)
    return _dot(sel_bf16, hi) + _dot(sel_bf16, mid) + _dot(sel_bf16, lo)


def _softplus(x):
    return jnp.maximum(x, 0.0) + jnp.log1p(jnp.exp(-jnp.abs(x)))


def _log_sigmoid(x):
    return jnp.minimum(x, 0.0) - jnp.log1p(jnp.exp(-jnp.abs(x)))


def _silu(x):
    return x * jax.nn.sigmoid(x)


def _rmsnorm_kernel(x_ref, w_ref, *rest):
    if len(rest) == 1:
        (o_ref,) = rest
    else:
        narrow_ref, o_ref, small_ref = rest
    x = x_ref[...]
    ms = jnp.mean(x * x, axis=-1, keepdims=True)
    y = (x * lax.rsqrt(ms + RMS_EPS) * w_ref[...]).astype(o_ref.dtype)
    o_ref[...] = y
    if len(rest) > 1:
        small_ref[...] = _dot_nt(y, narrow_ref[...].astype(BF16))


def _rmsnorm(x, w, out_dtype, tm=512, narrow_t=None):
    m, d = x.shape
    nbytes = 2 * tm * d * (4 + jnp.dtype(out_dtype).itemsize) + 2 * d * 4
    in_specs = [pl.BlockSpec((tm, d), lambda i: (i, 0)),
                pl.BlockSpec((1, d), lambda i: (0, 0))]
    out_specs = pl.BlockSpec((tm, d), lambda i: (i, 0))
    out_shape = jax.ShapeDtypeStruct((m, d), out_dtype)
    args = [x, w.reshape(1, d)]
    if narrow_t is not None:
        n = narrow_t.shape[0]
        in_specs.append(pl.BlockSpec((n, d), lambda i: (0, 0)))
        args.append(narrow_t)
        out_specs = [out_specs, pl.BlockSpec((tm, n), lambda i: (i, 0))]
        out_shape = [out_shape, jax.ShapeDtypeStruct((m, n), F32)]
        nbytes += 3 * n * d * 4 + 2 * tm * n * 4
    return pl.pallas_call(
        _rmsnorm_kernel,
        grid=(m // tm,),
        in_specs=in_specs,
        out_specs=out_specs,
        out_shape=out_shape,
        compiler_params=pltpu.CompilerParams(
            dimension_semantics=("arbitrary",), vmem_limit_bytes=_vmem_limit(nbytes)),
        name="rmsnorm",
    )(*args)


SIDE_BLOCKS = 128
PROJ_TM = 512
PROJ_TN = 1024


def _side_cast_specs(side, n_steps, ni):
    rows, cols = side.shape
    blk = rows // SIDE_BLOCKS
    assert rows % SIDE_BLOCKS == 0 and blk % (2 * SUBLANES) == 0 and n_steps >= SIDE_BLOCKS
    index = lambda j, i: (jnp.minimum(j * ni + i, SIDE_BLOCKS - 1), 0)
    spec = pl.BlockSpec((blk, cols), index)
    return spec, spec, jax.ShapeDtypeStruct((rows, cols), BF16), 2 * blk * cols * (4 + 2)


def _proj_kernel(*refs, w_is_transposed, relu_sq, has_side):
    if has_side:
        a_ref, w_ref, side_ref, o_ref, side_o_ref = refs
        side_o_ref[...] = side_ref[...].astype(BF16)
    else:
        a_ref, w_ref, o_ref = refs
    w = w_ref[...].astype(BF16)
    acc = _dot_nt(a_ref[...], w) if w_is_transposed else _dot(a_ref[...], w)
    if relu_sq:
        acc = jnp.square(jnp.maximum(acc, 0.0))
    o_ref[...] = acc.astype(o_ref.dtype)


def _proj(a, w, *, n_out, tm, tn, out_dtype, row0=None, relu_sq=False, side=None, name):
    m, kdim = a.shape
    assert n_out % tn == 0 and m % tm == 0
    nj, ni = n_out // tn, m // tm
    if row0 is None:
        w_spec = pl.BlockSpec((kdim, tn), lambda j, i: (0, j))
    else:
        assert row0 % SUBLANES == 0
        w_spec = pl.BlockSpec((pl.Element(tn), pl.Element(kdim)),
                              lambda j, i: (pl.multiple_of(row0 + j * tn, SUBLANES), 0))
    wbytes = w.dtype.itemsize
    nbytes = (2 * tm * kdim * 2 + 2 * kdim * tn * wbytes + kdim * tn * 2 * (wbytes > 2)
              + 2 * tm * tn * jnp.dtype(out_dtype).itemsize + tm * tn * 4)
    in_specs = [pl.BlockSpec((tm, kdim), lambda j, i: (i, 0)), w_spec]
    out_specs = pl.BlockSpec((tm, tn), lambda j, i: (i, j))
    out_shape = jax.ShapeDtypeStruct((m, n_out), out_dtype)
    args = [a, w]
    if side is not None:
        s_in, s_out, s_shape, s_bytes = _side_cast_specs(side, nj * ni, ni)
        in_specs.append(s_in)
        args.append(side)
        out_specs, out_shape = [out_specs, s_out], [out_shape, s_shape]
        nbytes += s_bytes
    return pl.pallas_call(
        functools.partial(_proj_kernel, w_is_transposed=row0 is not None, relu_sq=relu_sq,
                          has_side=side is not None),
        grid=(nj, ni),
        in_specs=in_specs,
        out_specs=out_specs,
        out_shape=out_shape,
        compiler_params=pltpu.CompilerParams(
            dimension_semantics=("arbitrary", "arbitrary"),
            vmem_limit_bytes=_vmem_limit(nbytes)),
        name=name,
    )(*args)


def _matmul_res_fullk_kernel(*refs, n_lhs, has_side):
    a_refs = refs[:n_lhs]
    if has_side:
        b_ref, res_ref, side_ref, o_ref, side_o_ref = refs[n_lhs:]
        side_o_ref[...] = side_ref[...].astype(BF16)
    else:
        b_ref, res_ref, o_ref = refs[n_lhs:]
    kpart = b_ref.shape[0] // n_lhs
    acc = _dot(a_refs[0][...], b_ref[0:kpart, :])
    for l in range(1, n_lhs):
        acc = acc + _dot(a_refs[l][...], b_ref[l * kpart:(l + 1) * kpart, :])
    o_ref[...] = (acc + res_ref[...]).astype(o_ref.dtype)


def _matmul_res_fullk(lhs, b, residual, *, tm, tn, side=None, name):
    n_lhs = len(lhs)
    m, kpart = lhs[0].shape
    kdim, n = b.shape
    nj, ni = n // tn, m // tm
    nbytes = (2 * n_lhs * tm * kpart + kdim * tn) * 2 + 4 * tm * tn * 4 + 2 * tm * tn * 4
    in_specs = ([pl.BlockSpec((tm, kpart), lambda j, i: (i, 0)) for _ in lhs]
                + [pl.BlockSpec((kdim, tn), lambda j, i: (0, j), pipeline_mode=pl.Buffered(1)),
                   pl.BlockSpec((tm, tn), lambda j, i: (i, j))])
    out_specs = pl.BlockSpec((tm, tn), lambda j, i: (i, j))
    out_shape = jax.ShapeDtypeStruct((m, n), residual.dtype)
    args = [*lhs, b, residual]
    if side is not None:
        s_in, s_out, s_shape, s_bytes = _side_cast_specs(side, nj * ni, ni)
        in_specs.append(s_in)
        args.append(side)
        out_specs, out_shape = [out_specs, s_out], [out_shape, s_shape]
        nbytes += s_bytes
    return pl.pallas_call(
        functools.partial(_matmul_res_fullk_kernel, n_lhs=n_lhs, has_side=side is not None),
        grid=(nj, ni),
        in_specs=in_specs,
        out_specs=out_specs,
        out_shape=out_shape,
        compiler_params=pltpu.CompilerParams(
            dimension_semantics=("arbitrary", "arbitrary"),
            vmem_limit_bytes=_vmem_limit(nbytes)),
        name=name,
    )(*args)


GLA_BLOCK = 256
GLA_HEADS_PER_STEP = 8


def _gla_kernel(q_ref, k_ref, v_ref, g_ref, sm_ref, w2_ref, gb_ref, nw_ref,
                o_ref, *state_refs):
    c = GLA_CHUNK
    dk, dv = GLA_DK, GLA_DV
    heads = range(GLA_HEADS_PER_STEP)
    kc = [slice(h * dk, (h + 1) * dk) for h in heads]
    vc = [slice(h * dv, (h + 1) * dv) for h in heads]

    @pl.when(pl.program_id(1) == 0)
    def _():
        for st in state_refs:
            st[...] = jnp.zeros_like(st)

    row = lax.broadcasted_iota(jnp.int32, (c, c), 0)
    col = lax.broadcasted_iota(jnp.int32, (c, c), 1)
    causal = row >= col
    tril = jnp.where(causal, 1.0, 0.0).astype(BF16)

    pre = _dot(sm_ref[...].astype(BF16), w2_ref[...]) + gb_ref[...]
    log_a = _log_sigmoid(pre) * (1.0 / GLA_GATE_NORMALIZER)

    for n in range(GLA_BLOCK // c):
        rows = slice(n * c, (n + 1) * c)
        b = [_select_dot(tril, log_a[rows, kc[h]]) for h in heads]
        b_last = [b[h][c - 1:c, :] for h in heads]
        qn = [q_ref[rows, kc[h]] * (GLA_DK ** -0.5) for h in heads]
        kn = [k_ref[rows, kc[h]] for h in heads]
        vn = [v_ref[rows, vc[h]].astype(BF16) for h in heads]
        q_dec = [(qn[h] * jnp.exp(b[h])).astype(BF16) for h in heads]
        k_dec = [(kn[h] * jnp.exp(-b[h])).astype(BF16) for h in heads]
        k_end = [(kn[h] * jnp.exp(b_last[h] - b[h])).astype(BF16) for h in heads]
        scores = [jnp.where(causal, _dot_nt(q_dec[h], k_dec[h]), 0.0) for h in heads]
        state = [state_refs[h][...] for h in heads]
        o = [_dot(scores[h].astype(BF16), vn[h]) + _dot_nt(q_dec[h], state[h].astype(BF16))
             for h in heads]
        for h in heads:
            state_refs[h][...] = state[h] * jnp.exp(b_last[h]) + _dot_tn(vn[h], k_end[h])
        ms = [jnp.mean(o[h] * o[h], axis=-1, keepdims=True) for h in heads]
        o = [o[h] * lax.rsqrt(ms[h] + RMS_EPS) * nw_ref[...] for h in heads]
        for h in heads:
            o_ref[rows, vc[h]] = (o[h] * _silu(g_ref[rows, vc[h]])).astype(o_ref.dtype)


def _gla_mixer(proj, small, w2_pad, gate_b, norm_w):
    s = proj.shape[0]
    t = GLA_BLOCK
    hp = GLA_HEADS_PER_STEP
    dkb, dvb = hp * GLA_DK, hp * GLA_DV
    nbytes = (2 * t * (2 * dkb + 2 * dvb + LANES) * 4 + 2 * t * dvb * 2
              + hp * GLA_DV * GLA_DK * 4 + 4 * t * dkb * 4)
    return pl.pallas_call(
        _gla_kernel,
        grid=(GLA_HEADS // hp, s // t),
        in_specs=[
            pl.BlockSpec((t, dkb), lambda h, i: (i, OFF_Q // dkb + h)),
            pl.BlockSpec((t, dkb), lambda h, i: (i, OFF_K // dkb + h)),
            pl.BlockSpec((t, dvb), lambda h, i: (i, OFF_V // dvb + h)),
            pl.BlockSpec((t, dvb), lambda h, i: (i, OFF_G // dvb + h)),
            pl.BlockSpec((t, LANES), lambda h, i: (i, 0)),
            pl.BlockSpec((LANES, dkb), lambda h, i: (0, h)),
            pl.BlockSpec((1, dkb), lambda h, i: (0, h)),
            pl.BlockSpec((1, GLA_DV), lambda h, i: (0, 0)),
        ],
        out_specs=pl.BlockSpec((t, dvb), lambda h, i: (i, h)),
        out_shape=jax.ShapeDtypeStruct((s, GLA_WIDTH), BF16),
        scratch_shapes=[pltpu.VMEM((GLA_DV, GLA_DK), F32) for _ in range(hp)],
        compiler_params=pltpu.CompilerParams(
            dimension_semantics=("arbitrary", "arbitrary"),
            vmem_limit_bytes=_vmem_limit(nbytes)),
        name="gla_mixer",
    )(proj, proj, proj, proj, small, w2_pad, gate_b.reshape(1, -1), norm_w.reshape(1, -1))


CONV_PAD = SUBLANES


def _ssd_kernel(x_ref, b_ref, c_ref, z_ref, sm_ref, cwx_ref, cwb_ref, cwc_ref,
                cbx_ref, cbb_ref, cbc_ref, dtb_ref, alog_ref, dsk_ref, nw_ref,
                o_ref, xpad_ref, bpad_ref, cpad_ref, cumt_ref, state_ref):
    L = SSD_CHUNK
    gw, n, hd = SSD_GROUP_WIDTH, SSD_STATE, SSD_HEAD_DIM
    groups = range(SSD_GROUPS)
    heads = range(SSD_HEADS_PER_GROUP)

    @pl.when(pl.program_id(0) == 0)
    def _():
        xpad_ref[0:CONV_PAD, :] = jnp.zeros((CONV_PAD, xpad_ref.shape[1]), F32)
        bpad_ref[0:CONV_PAD, :] = jnp.zeros((CONV_PAD, bpad_ref.shape[1]), F32)
        cpad_ref[0:CONV_PAD, :] = jnp.zeros((CONV_PAD, cpad_ref.shape[1]), F32)
        state_ref[...] = jnp.zeros_like(state_ref)

    def conv_silu(raw_ref, pad_ref, w_ref, bias_ref, cols):
        raw = raw_ref[:, cols]
        pad_ref[CONV_PAD:CONV_PAD + L, cols] = raw
        padded = pad_ref[:, cols]
        acc = bias_ref[:, cols] + w_ref[SSD_CONV - 1:SSD_CONV, cols] * raw
        for d in range(1, SSD_CONV):
            k = SSD_CONV - 1 - d
            shifted = pltpu.roll(padded, d, axis=0)[CONV_PAD:, :]
            acc = acc + w_ref[k:k + 1, cols] * shifted
        pad_ref[0:CONV_PAD, cols] = raw[L - CONV_PAD:, :]
        return _silu(acc)

    gcols = [slice(g * gw, (g + 1) * gw) for g in groups]
    ncols = [slice(g * n, (g + 1) * n) for g in groups]
    xs = [conv_silu(x_ref, xpad_ref, cwx_ref, cbx_ref, gcols[g]) for g in groups]
    bm = [conv_silu(b_ref, bpad_ref, cwb_ref, cbb_ref, ncols[g]).astype(BF16) for g in groups]
    cm = [conv_silu(c_ref, cpad_ref, cwc_ref, cbc_ref, ncols[g]).astype(BF16) for g in groups]

    dt = _softplus(sm_ref[...] + dtb_ref[...])
    a = dt * (-jnp.exp(alog_ref[...]))
    row = lax.broadcasted_iota(jnp.int32, (L, L), 0)
    col = lax.broadcasted_iota(jnp.int32, (L, L), 1)
    causal = row >= col
    tril = jnp.where(causal, 1.0, 0.0).astype(BF16)
    a_cum = _select_dot(tril, a)
    cumt_ref[...] = a_cum.T

    def lane_bcast(v, lane, width):
        return jnp.broadcast_to(v[:, lane:lane + 1], (L, width))

    def widen(v, g):
        base = SMALL_DT_OFF + g * SSD_HEADS_PER_GROUP
        return jnp.concatenate([lane_bcast(v, base + j, hd) for j in heads], axis=1)

    dt_x = [widen(dt, g) for g in groups]
    cum_x = [widen(a_cum, g) for g in groups]
    last_x = [cum_x[g][L - 1:L, :] for g in groups]
    xdt = [xs[g] * dt_x[g] for g in groups]
    xdt_bf = [xdt[g].astype(BF16) for g in groups]
    cb = [_dot_nt(cm[g], bm[g]) for g in groups]

    lane = lax.broadcasted_iota(jnp.int32, (L, 2 * hd), 1)
    first_half = lane < hd
    y_parts = [[] for _ in groups]
    for p in range(SSD_HEADS_PER_GROUP // 2):
        for g in groups:
            base = SMALL_DT_OFF + g * SSD_HEADS_PER_GROUP
            xpair = xdt_bf[g][:, p * 2 * hd:(p + 1) * 2 * hd]
            ys = []
            for j in (2 * p, 2 * p + 1):
                colb = lane_bcast(a_cum, base + j, L)
                rowb = cumt_ref[base + j:base + j + 1, :]
                decay = jnp.where(causal, jnp.exp(colb - rowb), 0.0)
                ys.append(_dot((cb[g] * decay).astype(BF16), xpair))
            y_parts[g].append(jnp.where(first_half, ys[0], ys[1]))
    y_diag = [jnp.concatenate(y_parts[g], axis=1) for g in groups]

    prev = [state_ref[:, gcols[g]] for g in groups]
    y_off = [_dot(cm[g], prev[g].astype(BF16)) * jnp.exp(cum_x[g]) for g in groups]
    xw = [(xdt[g] * jnp.exp(last_x[g] - cum_x[g])).astype(BF16) for g in groups]
    for g in groups:
        state_ref[:, gcols[g]] = prev[g] * jnp.exp(last_x[g]) + _dot_tn(bm[g], xw[g])

    y = [y_diag[g] + y_off[g] + xs[g] * dsk_ref[:, gcols[g]] for g in groups]
    yg = [y[g] * _silu(z_ref[:, gcols[g]]) for g in groups]
    ms = [jnp.mean(yg[g] * yg[g], axis=-1, keepdims=True) for g in groups]
    for g in groups:
        o_ref[:, gcols[g]] = (yg[g] * lax.rsqrt(ms[g] + RMS_EPS)
                              * nw_ref[:, gcols[g]]).astype(o_ref.dtype)


def _ssd_mixer(proj, small, conv_w, conv_b, dtb_pad, alog_pad, dskip_x, norm_w):
    s = proj.shape[0]
    L, w, gn = SSD_CHUNK, SSD_WIDTH, SSD_GROUPS * SSD_STATE
    full = lambda c: (0, 0)
    nbytes = (2 * L * (2 * w + 2 * gn + LANES) * 4 + 2 * L * w * 2
              + (L + CONV_PAD) * (w + 2 * gn) * 4 + LANES * L * 4 + SSD_STATE * w * 4
              + 8 * L * w * 4)
    return pl.pallas_call(
        _ssd_kernel,
        grid=(s // L,),
        in_specs=[
            pl.BlockSpec((L, w), lambda c: (c, OFF_X // w)),
            pl.BlockSpec((L, gn), lambda c: (c, OFF_B // gn)),
            pl.BlockSpec((L, gn), lambda c: (c, OFF_C // gn)),
            pl.BlockSpec((L, w), lambda c: (c, OFF_Z // w)),
            pl.BlockSpec((L, LANES), lambda c: (c, 0)),
            pl.BlockSpec((SSD_CONV, w), lambda c: (0, 0)),
            pl.BlockSpec((SSD_CONV, gn), lambda c: (0, w // gn)),
            pl.BlockSpec((SSD_CONV, gn), lambda c: (0, w // gn + 1)),
            pl.BlockSpec((1, w), lambda c: (0, 0)),
            pl.BlockSpec((1, gn), lambda c: (0, w // gn)),
            pl.BlockSpec((1, gn), lambda c: (0, w // gn + 1)),
            pl.BlockSpec((1, LANES), full),
            pl.BlockSpec((1, LANES), full),
            pl.BlockSpec((1, w), full),
            pl.BlockSpec((1, w), full),
        ],
        out_specs=pl.BlockSpec((L, w), lambda c: (c, 0)),
        out_shape=jax.ShapeDtypeStruct((s, w), BF16),
        scratch_shapes=[
            pltpu.VMEM((L + CONV_PAD, w), F32),
            pltpu.VMEM((L + CONV_PAD, gn), F32),
            pltpu.VMEM((L + CONV_PAD, gn), F32),
            pltpu.VMEM((LANES, L), F32),
            pltpu.VMEM((SSD_STATE, w), F32),
        ],
        compiler_params=pltpu.CompilerParams(
            dimension_semantics=("arbitrary",),
            vmem_limit_bytes=_vmem_limit(nbytes)),
        name="ssd_mixer",
    )(proj, proj, proj, proj, small,
      conv_w, conv_w, conv_w, conv_b, conv_b, conv_b,
      dtb_pad, alog_pad, dskip_x, norm_w.reshape(1, -1))


def kernel(x, norm_mix_w, w_in, gla_gate_w2, gla_gate_b, gla_norm_w, conv_w, conv_b,
           dt_bias, a_log, d_skip, ssd_norm_w, w_out, norm_mlp_w, w_up, w_down, norm_f_w):
    bsz, seq, d = x.shape
    h = x.reshape(bsz * seq, d)
    depth = w_in.shape[0]
    gate_lo = N_GLA
    z_lo = gate_lo + GLA_GATE_RANK
    dt_lo = z_lo + SSD_WIDTH + SSD_XBC
    pad_heads = lambda v: jnp.pad(v, (SMALL_DT_OFF, 0)).reshape(1, LANES)

    for l in range(depth):
        w_in_t = w_in[l].T
        w_small_t = jnp.concatenate(
            [w_in_t[gate_lo:z_lo],
             jnp.zeros((SMALL_DT_OFF - GLA_GATE_RANK, d), w_in.dtype),
             w_in_t[dt_lo:dt_lo + SSD_HEADS]], axis=0)
        w2_pad = jnp.pad(gla_gate_w2[l], ((0, LANES - GLA_GATE_RANK), (0, 0))).astype(BF16)
        dskip_x = jnp.repeat(d_skip[l], SSD_HEAD_DIM).reshape(1, SSD_WIDTH)

        u, small = _rmsnorm(h, norm_mix_w[l], BF16, narrow_t=w_small_t)
        proj_gla, w_out_bf = _proj(u, w_in_t, row0=0, n_out=N_GLA, tm=PROJ_TM, tn=PROJ_TN,
                                   out_dtype=F32, side=w_out[l], name="in_proj_gla")
        proj_ssd, w_up_bf = _proj(u, w_in_t, row0=z_lo, n_out=N_SSD, tm=PROJ_TM, tn=PROJ_TN,
                                  out_dtype=F32, side=w_up[l], name="in_proj_ssd")

        o_gla = _gla_mixer(proj_gla, small, w2_pad, gla_gate_b[l], gla_norm_w[l])
        o_ssd = _ssd_mixer(proj_ssd, small, conv_w[l], conv_b[l].reshape(1, -1),
                           pad_heads(dt_bias[l]), pad_heads(a_log[l]), dskip_x, ssd_norm_w[l])

        h = _matmul_res_fullk([o_gla, o_ssd], w_out_bf, h, tm=512, tn=1024, name="out_proj")
        u = _rmsnorm(h, norm_mlp_w[l], BF16)
        hid, w_down_bf = _proj(u, w_up_bf, n_out=w_up.shape[2], tm=1024, tn=1024,
                               out_dtype=BF16, relu_sq=True, side=w_down[l], name="mlp_up")
        h = _matmul_res_fullk([hid], w_down_bf, h, tm=256, tn=1024, name="mlp_down")

    out = _rmsnorm(h, norm_f_w, x.dtype)
    return out.reshape(bsz, seq, d)
```

```python
import functools

import jax
import jax.numpy as jnp
from jax import lax
from jax.experimental import pallas as pl
from jax.experimental.pallas import tpu as pltpu

F32 = jnp.float32
BF16 = jnp.bfloat16

D_MODEL = 4096
GLA_HEADS = 8
GLA_DK = 256
GLA_DV = 512
GLA_QK = GLA_HEADS * GLA_DK
GLA_WIDTH = GLA_HEADS * GLA_DV
GLA_GATE_RANK = 16
GLA_GATE_NORMALIZER = 16.0
GLA_CHUNK = 64
SSD_WIDTH = 4096
SSD_HEAD_DIM = 64
SSD_HEADS = SSD_WIDTH // SSD_HEAD_DIM
SSD_STATE = 128
SSD_GROUPS = 8
SSD_HEADS_PER_GROUP = SSD_HEADS // SSD_GROUPS
SSD_GROUP_WIDTH = SSD_HEADS_PER_GROUP * SSD_HEAD_DIM
SSD_CONV = 4
SSD_CHUNK = 128
SSD_XBC = SSD_WIDTH + 2 * SSD_GROUPS * SSD_STATE
RMS_EPS = 1e-5

LANES = 128
SUBLANES = 8
VMEM_BYTES_V7X = 64 * 1024 * 1024

OFF_Q = 0
OFF_K = OFF_Q + GLA_QK
OFF_V = OFF_K + GLA_QK
OFF_G = OFF_V + GLA_WIDTH
N_GLA = OFF_G + GLA_WIDTH
OFF_Z = 0
OFF_X = OFF_Z + SSD_WIDTH
OFF_B = OFF_X + SSD_WIDTH
OFF_C = OFF_B + SSD_GROUPS * SSD_STATE
N_SSD = OFF_C + SSD_GROUPS * SSD_STATE
SMALL_DT_OFF = LANES - SSD_HEADS


def _vmem_limit(nbytes):
    return int(min(nbytes + (8 << 20), VMEM_BYTES_V7X - (4 << 20)))


def _dot(a, b):
    return jnp.dot(a, b, preferred_element_type=F32)


def _dot_nt(a, b):
    return lax.dot_general(a, b, (((1,), (1,)), ((), ())), preferred_element_type=F32)


def _dot_tn(a, b):
    return lax.dot_general(a, b, (((0,), (0,)), ((), ())), preferred_element_type=F32)


def _split3(x):
    hi = x.astype(BF16)
    r1 = x - hi.astype(F32)
    mid = r1.astype(BF16)
    lo = (r1 - mid.astype(F32)).astype(BF16)
    return hi, mid, lo


def _select_dot(sel_bf16, x):
    hi, mid, lo = _split3(x)
    return _dot(sel_bf16, hi) + _dot(sel_bf16, mid) + _dot(sel_bf16, lo)


def _softplus(x):
    return jnp.maximum(x, 0.0) + jnp.log1p(jnp.exp(-jnp.abs(x)))


def _log_sigmoid(x):
    return jnp.minimum(x, 0.0) - jnp.log1p(jnp.exp(-jnp.abs(x)))


def _silu(x):
    return x * jax.nn.sigmoid(x)


def _rmsnorm_kernel(x_ref, w_ref, *rest):
    if len(rest) == 1:
        (o_ref,) = rest
    else:
        narrow_ref, o_ref, small_ref = rest
    x = x_ref[...]
    ms = jnp.mean(x * x, axis=-1, keepdims=True)
    y = (x * lax.rsqrt(ms + RMS_EPS) * w_ref[...]).astype(o_ref.dtype)
    o_ref[...] = y
    if len(rest) > 1:
        small_ref[...] = _dot_nt(y, narrow_ref[...].astype(BF16))


def _rmsnorm(x, w, out_dtype, tm=512, narrow_t=None):
    m, d = x.shape
    nbytes = 2 * tm * d * (4 + jnp.dtype(out_dtype).itemsize) + 2 * d * 4
    in_specs = [pl.BlockSpec((tm, d), lambda i: (i, 0)),
                pl.BlockSpec((1, d), lambda i: (0, 0))]
    out_specs = pl.BlockSpec((tm, d), lambda i: (i, 0))
    out_shape = jax.ShapeDtypeStruct((m, d), out_dtype)
    args = [x, w.reshape(1, d)]
    if narrow_t is not None:
        n = narrow_t.shape[0]
        in_specs.append(pl.BlockSpec((n, d), lambda i: (0, 0)))
        args.append(narrow_t)
        out_specs = [out_specs, pl.BlockSpec((tm, n), lambda i: (i, 0))]
        out_shape = [out_shape, jax.ShapeDtypeStruct((m, n), F32)]
        nbytes += 3 * n * d * 4 + 2 * tm * n * 4
    return pl.pallas_call(
        _rmsnorm_kernel,
        grid=(m // tm,),
        in_specs=in_specs,
        out_specs=out_specs,
        out_shape=out_shape,
        compiler_params=pltpu.CompilerParams(
            dimension_semantics=("arbitrary",), vmem_limit_bytes=_vmem_limit(nbytes)),
        name="rmsnorm",
    )(*args)


SIDE_BLOCKS = 128
PROJ_TM = 512
PROJ_TN = 1024


def _side_cast_specs(side, n_steps, step_of, blocks=SIDE_BLOCKS, row0=0, rows=None):
    rows = side.shape[0] if rows is None else rows
    cols = side.shape[1]
    blk = rows // blocks
    assert rows % blocks == 0 and blk % (2 * SUBLANES) == 0 and n_steps >= blocks
    assert row0 % SUBLANES == 0
    block_of = lambda *g: jnp.minimum(step_of(*g), blocks - 1)
    in_spec = pl.BlockSpec(
        (pl.Element(blk), pl.Element(cols)),
        lambda *g: (pl.multiple_of(row0 + block_of(*g) * blk, SUBLANES), 0))
    out_spec = pl.BlockSpec((blk, cols), lambda *g: (block_of(*g), 0))
    return in_spec, out_spec, jax.ShapeDtypeStruct((rows, cols), BF16), 2 * blk * cols * (4 + 2)


def _proj_kernel(*refs, w_is_transposed, relu_sq, n_side):
    a_ref, w_ref = refs[:2]
    o_ref = refs[2 + n_side]
    for s in range(n_side):
        refs[3 + n_side + s][...] = refs[2 + s][...].astype(BF16)
    w = w_ref[...].astype(BF16)
    acc = _dot_nt(a_ref[...], w) if w_is_transposed else _dot(a_ref[...], w)
    if relu_sq:
        acc = jnp.square(jnp.maximum(acc, 0.0))
    o_ref[...] = acc.astype(o_ref.dtype)


def _proj(a, w, *, n_out, tm, tn, out_dtype, row0=None, relu_sq=False, side=(), name):
    m, kdim = a.shape
    assert n_out % tn == 0 and m % tm == 0
    nj, ni = n_out // tn, m // tm
    if row0 is None:
        w_spec = pl.BlockSpec((kdim, tn), lambda j, i: (0, j))
    else:
        assert row0 % SUBLANES == 0
        w_spec = pl.BlockSpec((pl.Element(tn), pl.Element(kdim)),
                              lambda j, i: (pl.multiple_of(row0 + j * tn, SUBLANES), 0))
    wbytes = w.dtype.itemsize
    nbytes = (2 * tm * kdim * 2 + 2 * kdim * tn * wbytes + kdim * tn * 2 * (wbytes > 2)
              + 2 * tm * tn * jnp.dtype(out_dtype).itemsize + tm * tn * 4)
    in_specs = [pl.BlockSpec((tm, kdim), lambda j, i: (i, 0)), w_spec]
    out_specs = [pl.BlockSpec((tm, tn), lambda j, i: (i, j))]
    out_shape = [jax.ShapeDtypeStruct((m, n_out), out_dtype)]
    args = [a, w]
    for s_arr, s_row0, s_rows in side:
        s_in, s_out, s_shape, s_bytes = _side_cast_specs(
            s_arr, nj * ni, lambda j, i: j * ni + i, row0=s_row0, rows=s_rows)
        in_specs.append(s_in)
        args.append(s_arr)
        out_specs.append(s_out)
        out_shape.append(s_shape)
        nbytes += s_bytes
    results = pl.pallas_call(
        functools.partial(_proj_kernel, w_is_transposed=row0 is not None, relu_sq=relu_sq,
                          n_side=len(side)),
        grid=(nj, ni),
        in_specs=in_specs,
        out_specs=out_specs,
        out_shape=out_shape,
        compiler_params=pltpu.CompilerParams(
            dimension_semantics=("arbitrary", "arbitrary"),
            vmem_limit_bytes=_vmem_limit(nbytes)),
        name=name,
    )(*args)
    return results if side else results[0]


def _matmul_res_fullk_kernel(*refs, n_lhs):
    a_refs = refs[:n_lhs]
    b_ref, res_ref, o_ref = refs[n_lhs:]
    kpart = b_ref.shape[0] // n_lhs
    acc = _dot(a_refs[0][...], b_ref[0:kpart, :])
    for l in range(1, n_lhs):
        acc = acc + _dot(a_refs[l][...], b_ref[l * kpart:(l + 1) * kpart, :])
    o_ref[...] = (acc + res_ref[...]).astype(o_ref.dtype)


def _matmul_res_fullk(lhs, b, residual, *, tm, tn, name):
    n_lhs = len(lhs)
    m, kpart = lhs[0].shape
    kdim, n = b.shape
    nj, ni = n // tn, m // tm
    nbytes = (2 * n_lhs * tm * kpart + kdim * tn) * 2 + 4 * tm * tn * 4 + 2 * tm * tn * 4
    in_specs = ([pl.BlockSpec((tm, kpart), lambda j, i: (i, 0)) for _ in lhs]
                + [pl.BlockSpec((kdim, tn), lambda j, i: (0, j), pipeline_mode=pl.Buffered(1)),
                   pl.BlockSpec((tm, tn), lambda j, i: (i, j))])
    return pl.pallas_call(
        functools.partial(_matmul_res_fullk_kernel, n_lhs=n_lhs),
        grid=(nj, ni),
        in_specs=in_specs,
        out_specs=pl.BlockSpec((tm, tn), lambda j, i: (i, j)),
        out_shape=jax.ShapeDtypeStruct((m, n), residual.dtype),
        compiler_params=pltpu.CompilerParams(
            dimension_semantics=("arbitrary", "arbitrary"),
            vmem_limit_bytes=_vmem_limit(nbytes)),
        name=name,
    )(*lhs, b, residual)


GLA_BLOCK = 256
GLA_HEADS_PER_STEP = 8


def _gla_kernel(q_ref, k_ref, v_ref, g_ref, sm_ref, w2_ref, gb_ref, nw_ref,
                o_ref, *state_refs):
    c = GLA_CHUNK
    dk, dv = GLA_DK, GLA_DV
    heads = range(GLA_HEADS_PER_STEP)
    kc = [slice(h * dk, (h + 1) * dk) for h in heads]
    vc = [slice(h * dv, (h + 1) * dv) for h in heads]

    @pl.when(pl.program_id(1) == 0)
    def _():
        for st in state_refs:
            st[...] = jnp.zeros_like(st)

    row = lax.broadcasted_iota(jnp.int32, (c, c), 0)
    col = lax.broadcasted_iota(jnp.int32, (c, c), 1)
    causal = row >= col
    tril = jnp.where(causal, 1.0, 0.0).astype(BF16)

    pre = _dot(sm_ref[...].astype(BF16), w2_ref[...]) + gb_ref[...]
    log_a = _log_sigmoid(pre) * (1.0 / GLA_GATE_NORMALIZER)

    for n in range(GLA_BLOCK // c):
        rows = slice(n * c, (n + 1) * c)
        b = [_select_dot(tril, log_a[rows, kc[h]]) for h in heads]
        b_last = [b[h][c - 1:c, :] for h in heads]
        qn = [q_ref[rows, kc[h]] * (GLA_DK ** -0.5) for h in heads]
        kn = [k_ref[rows, kc[h]] for h in heads]
        vn = [v_ref[rows, vc[h]].astype(BF16) for h in heads]
        q_dec = [(qn[h] * jnp.exp(b[h])).astype(BF16) for h in heads]
        k_dec = [(kn[h] * jnp.exp(-b[h])).astype(BF16) for h in heads]
        k_end = [(kn[h] * jnp.exp(b_last[h] - b[h])).astype(BF16) for h in heads]
        scores = [jnp.where(causal, _dot_nt(q_dec[h], k_dec[h]), 0.0) for h in heads]
        state = [state_refs[h][...] for h in heads]
        o = [_dot(scores[h].astype(BF16), vn[h]) + _dot_nt(q_dec[h], state[h].astype(BF16))
             for h in heads]
        for h in heads:
            state_refs[h][...] = state[h] * jnp.exp(b_last[h]) + _dot_tn(vn[h], k_end[h])
        ms = [jnp.mean(o[h] * o[h], axis=-1, keepdims=True) for h in heads]
        o = [o[h] * lax.rsqrt(ms[h] + RMS_EPS) * nw_ref[...] for h in heads]
        for h in heads:
            o_ref[rows, vc[h]] = (o[h] * _silu(g_ref[rows, vc[h]])).astype(o_ref.dtype)


def _gla_mixer(proj, small, w2_pad, gate_b, norm_w):
    s = proj.shape[0]
    t = GLA_BLOCK
    hp = GLA_HEADS_PER_STEP
    dkb, dvb = hp * GLA_DK, hp * GLA_DV
    nbytes = (2 * t * (2 * dkb + 2 * dvb + LANES) * 4 + 2 * t * dvb * 2
              + hp * GLA_DV * GLA_DK * 4 + 4 * t * dkb * 4)
    return pl.pallas_call(
        _gla_kernel,
        grid=(GLA_HEADS // hp, s // t),
        in_specs=[
            pl.BlockSpec((t, dkb), lambda h, i: (i, OFF_Q // dkb + h)),
            pl.BlockSpec((t, dkb), lambda h, i: (i, OFF_K // dkb + h)),
            pl.BlockSpec((t, dvb), lambda h, i: (i, OFF_V // dvb + h)),
            pl.BlockSpec((t, dvb), lambda h, i: (i, OFF_G // dvb + h)),
            pl.BlockSpec((t, LANES), lambda h, i: (i, 0)),
            pl.BlockSpec((LANES, dkb), lambda h, i: (0, h)),
            pl.BlockSpec((1, dkb), lambda h, i: (0, h)),
            pl.BlockSpec((1, GLA_DV), lambda h, i: (0, 0)),
        ],
        out_specs=pl.BlockSpec((t, dvb), lambda h, i: (i, h)),
        out_shape=jax.ShapeDtypeStruct((s, GLA_WIDTH), BF16),
        scratch_shapes=[pltpu.VMEM((GLA_DV, GLA_DK), F32) for _ in range(hp)],
        compiler_params=pltpu.CompilerParams(
            dimension_semantics=("arbitrary", "arbitrary"),
            vmem_limit_bytes=_vmem_limit(nbytes)),
        name="gla_mixer",
    )(proj, proj, proj, proj, small, w2_pad, gate_b.reshape(1, -1), norm_w.reshape(1, -1))


CONV_PAD = SUBLANES


N_SSD_INPUTS = 15


def _ssd_kernel(*refs, has_side):
    (x_ref, b_ref, c_ref, z_ref, sm_ref, cwx_ref, cwb_ref, cwc_ref,
     cbx_ref, cbb_ref, cbc_ref, dtb_ref, alog_ref, dsk_ref, nw_ref) = refs[:N_SSD_INPUTS]
    rest = refs[N_SSD_INPUTS:]
    if has_side:
        side_ref, o_ref, side_o_ref = rest[:3]
        side_o_ref[...] = side_ref[...].astype(BF16)
        rest = rest[3:]
    else:
        o_ref = rest[0]
        rest = rest[1:]
    xpad_ref, bpad_ref, cpad_ref, cumt_ref, state_ref = rest
    L = SSD_CHUNK
    gw, n, hd = SSD_GROUP_WIDTH, SSD_STATE, SSD_HEAD_DIM
    groups = range(SSD_GROUPS)
    heads = range(SSD_HEADS_PER_GROUP)

    @pl.when(pl.program_id(0) == 0)
    def _():
        xpad_ref[0:CONV_PAD, :] = jnp.zeros((CONV_PAD, xpad_ref.shape[1]), F32)
        bpad_ref[0:CONV_PAD, :] = jnp.zeros((CONV_PAD, bpad_ref.shape[1]), F32)
        cpad_ref[0:CONV_PAD, :] = jnp.zeros((CONV_PAD, cpad_ref.shape[1]), F32)
        state_ref[...] = jnp.zeros_like(state_ref)

    def conv_silu(raw_ref, pad_ref, w_ref, bias_ref, cols):
        raw = raw_ref[:, cols]
        pad_ref[CONV_PAD:CONV_PAD + L, cols] = raw
        padded = pad_ref[:, cols]
        acc = bias_ref[:, cols] + w_ref[SSD_CONV - 1:SSD_CONV, cols] * raw
        for d in range(1, SSD_CONV):
            k = SSD_CONV - 1 - d
            shifted = pltpu.roll(padded, d, axis=0)[CONV_PAD:, :]
            acc = acc + w_ref[k:k + 1, cols] * shifted
        pad_ref[0:CONV_PAD, cols] = raw[L - CONV_PAD:, :]
        return _silu(acc)

    gcols = [slice(g * gw, (g + 1) * gw) for g in groups]
    ncols = [slice(g * n, (g + 1) * n) for g in groups]
    xs = [conv_silu(x_ref, xpad_ref, cwx_ref, cbx_ref, gcols[g]) for g in groups]
    bm = [conv_silu(b_ref, bpad_ref, cwb_ref, cbb_ref, ncols[g]).astype(BF16) for g in groups]
    cm = [conv_silu(c_ref, cpad_ref, cwc_ref, cbc_ref, ncols[g]).astype(BF16) for g in groups]

    dt = _softplus(sm_ref[...] + dtb_ref[...])
    a = dt * (-jnp.exp(alog_ref[...]))
    row = lax.broadcasted_iota(jnp.int32, (L, L), 0)
    col = lax.broadcasted_iota(jnp.int32, (L, L), 1)
    causal = row >= col
    tril = jnp.where(causal, 1.0, 0.0).astype(BF16)
    a_cum = _select_dot(tril, a)
    cumt_ref[...] = a_cum.T

    def lane_bcast(v, lane, width):
        return jnp.broadcast_to(v[:, lane:lane + 1], (L, width))

    def widen(v, g):
        base = SMALL_DT_OFF + g * SSD_HEADS_PER_GROUP
        return jnp.concatenate([lane_bcast(v, base + j, hd) for j in heads], axis=1)

    dt_x = [widen(dt, g) for g in groups]
    cum_x = [widen(a_cum, g) for g in groups]
    last_x = [cum_x[g][L - 1:L, :] for g in groups]
    xdt = [xs[g] * dt_x[g] for g in groups]
    xdt_bf = [xdt[g].astype(BF16) for g in groups]
    cb = [_dot_nt(cm[g], bm[g]) for g in groups]

    lane = lax.broadcasted_iota(jnp.int32, (L, 2 * hd), 1)
    first_half = lane < hd
    y_parts = [[] for _ in groups]
    for p in range(SSD_HEADS_PER_GROUP // 2):
        for g in groups:
            base = SMALL_DT_OFF + g * SSD_HEADS_PER_GROUP
            xpair = xdt_bf[g][:, p * 2 * hd:(p + 1) * 2 * hd]
            ys = []
            for j in (2 * p, 2 * p + 1):
                colb = lane_bcast(a_cum, base + j, L)
                rowb = cumt_ref[base + j:base + j + 1, :]
                decay = jnp.where(causal, jnp.exp(colb - rowb), 0.0)
                ys.append(_dot((cb[g] * decay).astype(BF16), xpair))
            y_parts[g].append(jnp.where(first_half, ys[0], ys[1]))
    y_diag = [jnp.concatenate(y_parts[g], axis=1) for g in groups]

    prev = [state_ref[:, gcols[g]] for g in groups]
    y_off = [_dot(cm[g], prev[g].astype(BF16)) * jnp.exp(cum_x[g]) for g in groups]
    xw = [(xdt[g] * jnp.exp(last_x[g] - cum_x[g])).astype(BF16) for g in groups]
    for g in groups:
        state_ref[:, gcols[g]] = prev[g] * jnp.exp(last_x[g]) + _dot_tn(bm[g], xw[g])

    y = [y_diag[g] + y_off[g] + xs[g] * dsk_ref[:, gcols[g]] for g in groups]
    yg = [y[g] * _silu(z_ref[:, gcols[g]]) for g in groups]
    ms = [jnp.mean(yg[g] * yg[g], axis=-1, keepdims=True) for g in groups]
    for g in groups:
        o_ref[:, gcols[g]] = (yg[g] * lax.rsqrt(ms[g] + RMS_EPS)
                              * nw_ref[:, gcols[g]]).astype(o_ref.dtype)


def _ssd_mixer(proj, small, conv_w, conv_b, dtb_pad, alog_pad, dskip_x, norm_w, side):
    s = proj.shape[0]
    L, w, gn = SSD_CHUNK, SSD_WIDTH, SSD_GROUPS * SSD_STATE
    full = lambda c: (0, 0)
    n_steps = s // L
    s_in, s_out, s_shape, s_bytes = _side_cast_specs(side, n_steps, lambda c: c, blocks=n_steps)
    nbytes = (2 * L * (2 * w + 2 * gn + LANES) * 4 + 2 * L * w * 2
              + (L + CONV_PAD) * (w + 2 * gn) * 4 + LANES * L * 4 + SSD_STATE * w * 4
              + 8 * L * w * 4 + s_bytes)
    return pl.pallas_call(
        functools.partial(_ssd_kernel, has_side=True),
        grid=(n_steps,),
        in_specs=[
            pl.BlockSpec((L, w), lambda c: (c, OFF_X // w)),
            pl.BlockSpec((L, gn), lambda c: (c, OFF_B // gn)),
            pl.BlockSpec((L, gn), lambda c: (c, OFF_C // gn)),
            pl.BlockSpec((L, w), lambda c: (c, OFF_Z // w)),
            pl.BlockSpec((L, LANES), lambda c: (c, 0)),
            pl.BlockSpec((SSD_CONV, w), lambda c: (0, 0)),
            pl.BlockSpec((SSD_CONV, gn), lambda c: (0, w // gn)),
            pl.BlockSpec((SSD_CONV, gn), lambda c: (0, w // gn + 1)),
            pl.BlockSpec((1, w), lambda c: (0, 0)),
            pl.BlockSpec((1, gn), lambda c: (0, w // gn)),
            pl.BlockSpec((1, gn), lambda c: (0, w // gn + 1)),
            pl.BlockSpec((1, LANES), full),
            pl.BlockSpec((1, LANES), full),
            pl.BlockSpec((1, w), full),
            pl.BlockSpec((1, w), full),
            s_in,
        ],
        out_specs=[pl.BlockSpec((L, w), lambda c: (c, 0)), s_out],
        out_shape=[jax.ShapeDtypeStruct((s, w), BF16), s_shape],
        scratch_shapes=[
            pltpu.VMEM((L + CONV_PAD, w), F32),
            pltpu.VMEM((L + CONV_PAD, gn), F32),
            pltpu.VMEM((L + CONV_PAD, gn), F32),
            pltpu.VMEM((LANES, L), F32),
            pltpu.VMEM((SSD_STATE, w), F32),
        ],
        compiler_params=pltpu.CompilerParams(
            dimension_semantics=("arbitrary",),
            vmem_limit_bytes=_vmem_limit(nbytes)),
        name="ssd_mixer",
    )(proj, proj, proj, proj, small,
      conv_w, conv_w, conv_w, conv_b, conv_b, conv_b,
      dtb_pad, alog_pad, dskip_x, norm_w.reshape(1, -1), side)


def kernel(x, norm_mix_w, w_in, gla_gate_w2, gla_gate_b, gla_norm_w, conv_w, conv_b,
           dt_bias, a_log, d_skip, ssd_norm_w, w_out, norm_mlp_w, w_up, w_down, norm_f_w):
    bsz, seq, d = x.shape
    h = x.reshape(bsz * seq, d)
    depth = w_in.shape[0]
    gate_lo = N_GLA
    z_lo = gate_lo + GLA_GATE_RANK
    dt_lo = z_lo + SSD_WIDTH + SSD_XBC
    pad_heads = lambda v: jnp.pad(v, (SMALL_DT_OFF, 0)).reshape(1, LANES)

    for l in range(depth):
        w_in_t = w_in[l].T
        w_small_t = jnp.concatenate(
            [w_in_t[gate_lo:z_lo],
             jnp.zeros((SMALL_DT_OFF - GLA_GATE_RANK, d), w_in.dtype),
             w_in_t[dt_lo:dt_lo + SSD_HEADS]], axis=0)
        w2_pad = jnp.pad(gla_gate_w2[l], ((0, LANES - GLA_GATE_RANK), (0, 0))).astype(BF16)
        dskip_x = jnp.repeat(d_skip[l], SSD_HEAD_DIM).reshape(1, SSD_WIDTH)

        u, small = _rmsnorm(h, norm_mix_w[l], BF16, narrow_t=w_small_t)
        proj_gla, w_out_bf, w_ssd_bf = _proj(
            u, w_in_t, row0=0, n_out=N_GLA, tm=PROJ_TM, tn=PROJ_TN, out_dtype=F32,
            side=((w_out[l], 0, w_out.shape[1]), (w_in_t, z_lo, N_SSD)), name="in_proj_gla")
        proj_ssd = _proj(u, w_ssd_bf, row0=0, n_out=N_SSD, tm=1024, tn=1024,
                         out_dtype=F32, name="in_proj_ssd")

        o_gla = _gla_mixer(proj_gla, small, w2_pad, gla_gate_b[l], gla_norm_w[l])
        o_ssd, w_up_bf = _ssd_mixer(proj_ssd, small, conv_w[l], conv_b[l].reshape(1, -1),
                                    pad_heads(dt_bias[l]), pad_heads(a_log[l]), dskip_x,
                                    ssd_norm_w[l], w_up[l])

        h = _matmul_res_fullk([o_gla, o_ssd], w_out_bf, h, tm=512, tn=1024, name="out_proj")
        u = _rmsnorm(h, norm_mlp_w[l], BF16)
        hid, w_down_bf = _proj(u, w_up_bf, n_out=w_up.shape[2], tm=1024, tn=1024,
                               out_dtype=BF16, relu_sq=True,
                               side=((w_down[l], 0, w_down.shape[1]),), name="mlp_up")
        h = _matmul_res_fullk([hid], w_down_bf, h, tm=256, tn=1024, name="mlp_down")

    out = _rmsnorm(h, norm_f_w, x.dtype)
    return out.reshape(bsz, seq, d)
```

```python
import functools

import jax
import jax.numpy as jnp
from jax import lax
from jax.experimental import pallas as pl
from jax.experimental.pallas import tpu as pltpu

F32 = jnp.float32
BF16 = jnp.bfloat16

D_MODEL = 4096
GLA_HEADS = 8
GLA_DK = 256
GLA_DV = 512
GLA_QK = GLA_HEADS * GLA_DK
GLA_WIDTH = GLA_HEADS * GLA_DV
GLA_GATE_RANK = 16
GLA_GATE_NORMALIZER = 16.0
GLA_CHUNK = 64
SSD_WIDTH = 4096
SSD_HEAD_DIM = 64
SSD_HEADS = SSD_WIDTH // SSD_HEAD_DIM
SSD_STATE = 128
SSD_GROUPS = 8
SSD_HEADS_PER_GROUP = SSD_HEADS // SSD_GROUPS
SSD_GROUP_WIDTH = SSD_HEADS_PER_GROUP * SSD_HEAD_DIM
SSD_CONV = 4
SSD_CHUNK = 128
SSD_XBC = SSD_WIDTH + 2 * SSD_GROUPS * SSD_STATE
RMS_EPS = 1e-5

LANES = 128
SUBLANES = 8
VMEM_BYTES_V7X = 64 * 1024 * 1024

OFF_Q = 0
OFF_K = OFF_Q + GLA_QK
OFF_V = OFF_K + GLA_QK
OFF_G = OFF_V + GLA_WIDTH
N_GLA = OFF_G + GLA_WIDTH
OFF_Z = 0
OFF_X = OFF_Z + SSD_WIDTH
OFF_B = OFF_X + SSD_WIDTH
OFF_C = OFF_B + SSD_GROUPS * SSD_STATE
N_SSD = OFF_C + SSD_GROUPS * SSD_STATE
SMALL_DT_OFF = LANES - SSD_HEADS


def _vmem_limit(nbytes):
    return int(min(nbytes + (8 << 20), VMEM_BYTES_V7X - (4 << 20)))


def _dot(a, b):
    return jnp.dot(a, b, preferred_element_type=F32)


def _dot_nt(a, b):
    return lax.dot_general(a, b, (((1,), (1,)), ((), ())), preferred_element_type=F32)


def _dot_tn(a, b):
    return lax.dot_general(a, b, (((0,), (0,)), ((), ())), preferred_element_type=F32)


def _split3(x):
    hi = x.astype(BF16)
    r1 = x - hi.astype(F32)
    mid = r1.astype(BF16)
    lo = (r1 - mid.astype(F32)).astype(BF16)
    return hi, mid, lo


def _select_dot(sel_bf16, x):
    hi, mid, lo = _split3(x)
    return _dot(sel_bf16, hi) + _dot(sel_bf16, mid) + _dot(sel_bf16, lo)


def _softplus(x):
    return jnp.maximum(x, 0.0) + jnp.log1p(jnp.exp(-jnp.abs(x)))


def _log_sigmoid(x):
    return jnp.minimum(x, 0.0) - jnp.log1p(jnp.exp(-jnp.abs(x)))


def _silu(x):
    return x * jax.nn.sigmoid(x)


def _rmsnorm_kernel(x_ref, w_ref, *rest):
    if len(rest) == 1:
        (o_ref,) = rest
    else:
        narrow_ref, o_ref, small_ref = rest
    x = x_ref[...]
    ms = jnp.mean(x * x, axis=-1, keepdims=True)
    y = (x * lax.rsqrt(ms + RMS_EPS) * w_ref[...]).astype(o_ref.dtype)
    o_ref[...] = y
    if len(rest) > 1:
        small_ref[...] = _dot_nt(y, narrow_ref[...].astype(BF16))


def _rmsnorm(x, w, out_dtype, tm=512, narrow_t=None):
    m, d = x.shape
    nbytes = 2 * tm * d * (4 + jnp.dtype(out_dtype).itemsize) + 2 * d * 4
    in_specs = [pl.BlockSpec((tm, d), lambda i: (i, 0)),
                pl.BlockSpec((1, d), lambda i: (0, 0))]
    out_specs = pl.BlockSpec((tm, d), lambda i: (i, 0))
    out_shape = jax.ShapeDtypeStruct((m, d), out_dtype)
    args = [x, w.reshape(1, d)]
    if narrow_t is not None:
        n = narrow_t.shape[0]
        in_specs.append(pl.BlockSpec((n, d), lambda i: (0, 0)))
        args.append(narrow_t)
        out_specs = [out_specs, pl.BlockSpec((tm, n), lambda i: (i, 0))]
        out_shape = [out_shape, jax.ShapeDtypeStruct((m, n), F32)]
        nbytes += 3 * n * d * 4 + 2 * tm * n * 4
    return pl.pallas_call(
        _rmsnorm_kernel,
        grid=(m // tm,),
        in_specs=in_specs,
        out_specs=out_specs,
        out_shape=out_shape,
        compiler_params=pltpu.CompilerParams(
            dimension_semantics=("arbitrary",), vmem_limit_bytes=_vmem_limit(nbytes)),
        name="rmsnorm",
    )(*args)


SIDE_BLOCKS = 128
PROJ_TM = 512
PROJ_TN = 1024


def _side_cast_specs(side, n_steps, step_of, blocks=SIDE_BLOCKS, row0=0, rows=None):
    rows = side.shape[0] if rows is None else rows
    cols = side.shape[1]
    blk = rows // blocks
    assert rows % blocks == 0 and blk % (2 * SUBLANES) == 0 and n_steps >= blocks
    assert row0 % SUBLANES == 0
    block_of = lambda *g: jnp.minimum(step_of(*g), blocks - 1)
    in_spec = pl.BlockSpec(
        (pl.Element(blk), pl.Element(cols)),
        lambda *g: (pl.multiple_of(row0 + block_of(*g) * blk, SUBLANES), 0))
    out_spec = pl.BlockSpec((blk, cols), lambda *g: (block_of(*g), 0))
    return in_spec, out_spec, jax.ShapeDtypeStruct((rows, cols), BF16), 2 * blk * cols * (4 + 2)


def _proj_kernel(*refs, w_is_transposed, relu_sq, n_side):
    a_ref, w_ref = refs[:2]
    o_ref = refs[2 + n_side]
    for s in range(n_side):
        refs[3 + n_side + s][...] = refs[2 + s][...].astype(BF16)
    w = w_ref[...].astype(BF16)
    acc = _dot_nt(a_ref[...], w) if w_is_transposed else _dot(a_ref[...], w)
    if relu_sq:
        acc = jnp.square(jnp.maximum(acc, 0.0))
    o_ref[...] = acc.astype(o_ref.dtype)


def _proj(a, w, *, n_out, tm, tn, out_dtype, row0=None, relu_sq=False, side=(), name):
    m, kdim = a.shape
    assert n_out % tn == 0 and m % tm == 0
    nj, ni = n_out // tn, m // tm
    if row0 is None:
        w_spec = pl.BlockSpec((kdim, tn), lambda j, i: (0, j))
    else:
        assert row0 % SUBLANES == 0
        w_spec = pl.BlockSpec((pl.Element(tn), pl.Element(kdim)),
                              lambda j, i: (pl.multiple_of(row0 + j * tn, SUBLANES), 0))
    wbytes = w.dtype.itemsize
    nbytes = (2 * tm * kdim * 2 + 2 * kdim * tn * wbytes + kdim * tn * 2 * (wbytes > 2)
              + 2 * tm * tn * jnp.dtype(out_dtype).itemsize + tm * tn * 4)
    in_specs = [pl.BlockSpec((tm, kdim), lambda j, i: (i, 0)), w_spec]
    out_specs = [pl.BlockSpec((tm, tn), lambda j, i: (i, j))]
    out_shape = [jax.ShapeDtypeStruct((m, n_out), out_dtype)]
    args = [a, w]
    for s_arr, s_row0, s_rows in side:
        s_in, s_out, s_shape, s_bytes = _side_cast_specs(
            s_arr, nj * ni, lambda j, i: j * ni + i, row0=s_row0, rows=s_rows)
        in_specs.append(s_in)
        args.append(s_arr)
        out_specs.append(s_out)
        out_shape.append(s_shape)
        nbytes += s_bytes
    results = pl.pallas_call(
        functools.partial(_proj_kernel, w_is_transposed=row0 is not None, relu_sq=relu_sq,
                          n_side=len(side)),
        grid=(nj, ni),
        in_specs=in_specs,
        out_specs=out_specs,
        out_shape=out_shape,
        compiler_params=pltpu.CompilerParams(
            dimension_semantics=("arbitrary", "arbitrary"),
            vmem_limit_bytes=_vmem_limit(nbytes)),
        name=name,
    )(*args)
    return results if side else results[0]


def _matmul_res_fullk_kernel(*refs, n_lhs):
    a_refs = refs[:n_lhs]
    b_ref, res_ref, o_ref = refs[n_lhs:]
    kpart = b_ref.shape[0] // n_lhs
    acc = _dot(a_refs[0][...], b_ref[0:kpart, :])
    for l in range(1, n_lhs):
        acc = acc + _dot(a_refs[l][...], b_ref[l * kpart:(l + 1) * kpart, :])
    o_ref[...] = (acc + res_ref[...]).astype(o_ref.dtype)


def _matmul_res_fullk(lhs, b, residual, *, tm, tn, name):
    n_lhs = len(lhs)
    m, kpart = lhs[0].shape
    kdim, n = b.shape
    nj, ni = n // tn, m // tm
    nbytes = (2 * n_lhs * tm * kpart + kdim * tn) * 2 + 4 * tm * tn * 4 + 2 * tm * tn * 4
    in_specs = ([pl.BlockSpec((tm, kpart), lambda j, i: (i, 0)) for _ in lhs]
                + [pl.BlockSpec((kdim, tn), lambda j, i: (0, j), pipeline_mode=pl.Buffered(1)),
                   pl.BlockSpec((tm, tn), lambda j, i: (i, j))])
    return pl.pallas_call(
        functools.partial(_matmul_res_fullk_kernel, n_lhs=n_lhs),
        grid=(nj, ni),
        in_specs=in_specs,
        out_specs=pl.BlockSpec((tm, tn), lambda j, i: (i, j)),
        out_shape=jax.ShapeDtypeStruct((m, n), residual.dtype),
        compiler_params=pltpu.CompilerParams(
            dimension_semantics=("arbitrary", "arbitrary"),
            vmem_limit_bytes=_vmem_limit(nbytes)),
        name=name,
    )(*lhs, b, residual)


GLA_BLOCK = 256
GLA_HEADS_PER_STEP = 8


def _gla_kernel(q_ref, k_ref, v_ref, g_ref, sm_ref, w2_ref, gb_ref, nw_ref,
                o_ref, *state_refs):
    c = GLA_CHUNK
    dk, dv = GLA_DK, GLA_DV
    heads = range(GLA_HEADS_PER_STEP)
    kc = [slice(h * dk, (h + 1) * dk) for h in heads]
    vc = [slice(h * dv, (h + 1) * dv) for h in heads]

    @pl.when(pl.program_id(1) == 0)
    def _():
        for st in state_refs:
            st[...] = jnp.zeros_like(st)

    row = lax.broadcasted_iota(jnp.int32, (c, c), 0)
    col = lax.broadcasted_iota(jnp.int32, (c, c), 1)
    causal = row >= col
    tril = jnp.where(causal, 1.0, 0.0).astype(BF16)

    for n in range(GLA_BLOCK // c):
        rows = slice(n * c, (n + 1) * c)
        pre = _dot(sm_ref[rows, :].astype(BF16), w2_ref[...]) + gb_ref[...]
        log_a = _log_sigmoid(pre) * (1.0 / GLA_GATE_NORMALIZER)
        b = [_select_dot(tril, log_a[:, kc[h]]) for h in heads]
        b_last = [b[h][c - 1:c, :] for h in heads]
        qn = [q_ref[rows, kc[h]] * (GLA_DK ** -0.5) for h in heads]
        kn = [k_ref[rows, kc[h]] for h in heads]
        vn = [v_ref[rows, vc[h]].astype(BF16) for h in heads]
        q_dec = [(qn[h] * jnp.exp(b[h])).astype(BF16) for h in heads]
        k_dec = [(kn[h] * jnp.exp(-b[h])).astype(BF16) for h in heads]
        k_end = [(kn[h] * jnp.exp(b_last[h] - b[h])).astype(BF16) for h in heads]
        scores = [jnp.where(causal, _dot_nt(q_dec[h], k_dec[h]), 0.0) for h in heads]
        state = [state_refs[h][...] for h in heads]
        o = [_dot(scores[h].astype(BF16), vn[h]) + _dot_nt(q_dec[h], state[h].astype(BF16))
             for h in heads]
        for h in heads:
            state_refs[h][...] = state[h] * jnp.exp(b_last[h]) + _dot_tn(vn[h], k_end[h])
        ms = [jnp.mean(o[h] * o[h], axis=-1, keepdims=True) for h in heads]
        o = [o[h] * lax.rsqrt(ms[h] + RMS_EPS) * nw_ref[...] for h in heads]
        for h in heads:
            o_ref[rows, vc[h]] = (o[h] * _silu(g_ref[rows, vc[h]])).astype(o_ref.dtype)


def _gla_mixer(proj, small, w2_pad, gate_b, norm_w):
    s = proj.shape[0]
    t = GLA_BLOCK
    hp = GLA_HEADS_PER_STEP
    dkb, dvb = hp * GLA_DK, hp * GLA_DV
    nbytes = (2 * t * (2 * dkb + 2 * dvb + LANES) * 4 + 2 * t * dvb * 2
              + hp * GLA_DV * GLA_DK * 4 + 4 * t * dkb * 4)
    return pl.pallas_call(
        _gla_kernel,
        grid=(GLA_HEADS // hp, s // t),
        in_specs=[
            pl.BlockSpec((t, dkb), lambda h, i: (i, OFF_Q // dkb + h)),
            pl.BlockSpec((t, dkb), lambda h, i: (i, OFF_K // dkb + h)),
            pl.BlockSpec((t, dvb), lambda h, i: (i, OFF_V // dvb + h)),
            pl.BlockSpec((t, dvb), lambda h, i: (i, OFF_G // dvb + h)),
            pl.BlockSpec((t, LANES), lambda h, i: (i, 0)),
            pl.BlockSpec((LANES, dkb), lambda h, i: (0, h)),
            pl.BlockSpec((1, dkb), lambda h, i: (0, h)),
            pl.BlockSpec((1, GLA_DV), lambda h, i: (0, 0)),
        ],
        out_specs=pl.BlockSpec((t, dvb), lambda h, i: (i, h)),
        out_shape=jax.ShapeDtypeStruct((s, GLA_WIDTH), BF16),
        scratch_shapes=[pltpu.VMEM((GLA_DV, GLA_DK), F32) for _ in range(hp)],
        compiler_params=pltpu.CompilerParams(
            dimension_semantics=("arbitrary", "arbitrary"),
            vmem_limit_bytes=_vmem_limit(nbytes)),
        name="gla_mixer",
    )(proj, proj, proj, proj, small, w2_pad, gate_b.reshape(1, -1), norm_w.reshape(1, -1))


CONV_PAD = SUBLANES


N_SSD_INPUTS = 15


def _ssd_kernel(*refs, has_side):
    (x_ref, b_ref, c_ref, z_ref, sm_ref, cwx_ref, cwb_ref, cwc_ref,
     cbx_ref, cbb_ref, cbc_ref, dtb_ref, alog_ref, dsk_ref, nw_ref) = refs[:N_SSD_INPUTS]
    rest = refs[N_SSD_INPUTS:]
    if has_side:
        side_ref, o_ref, side_o_ref = rest[:3]
        side_o_ref[...] = side_ref[...].astype(BF16)
        rest = rest[3:]
    else:
        o_ref = rest[0]
        rest = rest[1:]
    xpad_ref, bpad_ref, cpad_ref, cumt_ref, state_ref = rest
    L = SSD_CHUNK
    gw, n, hd = SSD_GROUP_WIDTH, SSD_STATE, SSD_HEAD_DIM
    groups = range(SSD_GROUPS)
    heads = range(SSD_HEADS_PER_GROUP)

    @pl.when(pl.program_id(0) == 0)
    def _():
        xpad_ref[0:CONV_PAD, :] = jnp.zeros((CONV_PAD, xpad_ref.shape[1]), F32)
        bpad_ref[0:CONV_PAD, :] = jnp.zeros((CONV_PAD, bpad_ref.shape[1]), F32)
        cpad_ref[0:CONV_PAD, :] = jnp.zeros((CONV_PAD, cpad_ref.shape[1]), F32)
        state_ref[...] = jnp.zeros_like(state_ref)

    def conv_silu(raw_ref, pad_ref, w_ref, bias_ref, cols):
        raw = raw_ref[:, cols]
        pad_ref[CONV_PAD:CONV_PAD + L, cols] = raw
        padded = pad_ref[:, cols]
        acc = bias_ref[:, cols] + w_ref[SSD_CONV - 1:SSD_CONV, cols] * raw
        for d in range(1, SSD_CONV):
            k = SSD_CONV - 1 - d
            shifted = pltpu.roll(padded, d, axis=0)[CONV_PAD:, :]
            acc = acc + w_ref[k:k + 1, cols] * shifted
        pad_ref[0:CONV_PAD, cols] = raw[L - CONV_PAD:, :]
        return _silu(acc)

    gcols = [slice(g * gw, (g + 1) * gw) for g in groups]
    ncols = [slice(g * n, (g + 1) * n) for g in groups]
    xs = [conv_silu(x_ref, xpad_ref, cwx_ref, cbx_ref, gcols[g]) for g in groups]
    bm = [conv_silu(b_ref, bpad_ref, cwb_ref, cbb_ref, ncols[g]).astype(BF16) for g in groups]
    cm = [conv_silu(c_ref, cpad_ref, cwc_ref, cbc_ref, ncols[g]).astype(BF16) for g in groups]

    dt = _softplus(sm_ref[...] + dtb_ref[...])
    a = dt * (-jnp.exp(alog_ref[...]))
    row = lax.broadcasted_iota(jnp.int32, (L, L), 0)
    col = lax.broadcasted_iota(jnp.int32, (L, L), 1)
    causal = row >= col
    tril = jnp.where(causal, 1.0, 0.0).astype(BF16)
    a_cum = _select_dot(tril, a)
    cumt_ref[...] = a_cum.T

    def lane_bcast(v, lane, width):
        return jnp.broadcast_to(v[:, lane:lane + 1], (L, width))

    def widen(v, g):
        base = SMALL_DT_OFF + g * SSD_HEADS_PER_GROUP
        return jnp.concatenate([lane_bcast(v, base + j, hd) for j in heads], axis=1)

    dt_x = [widen(dt, g) for g in groups]
    cum_x = [widen(a_cum, g) for g in groups]
    last_x = [cum_x[g][L - 1:L, :] for g in groups]
    xdt = [xs[g] * dt_x[g] for g in groups]
    xdt_bf = [xdt[g].astype(BF16) for g in groups]
    cb = [_dot_nt(cm[g], bm[g]) for g in groups]

    lane = lax.broadcasted_iota(jnp.int32, (L, 2 * hd), 1)
    first_half = lane < hd
    y_parts = [[] for _ in groups]
    for p in range(SSD_HEADS_PER_GROUP // 2):
        for g in groups:
            base = SMALL_DT_OFF + g * SSD_HEADS_PER_GROUP
            xpair = xdt_bf[g][:, p * 2 * hd:(p + 1) * 2 * hd]
            ys = []
            for j in (2 * p, 2 * p + 1):
                colb = lane_bcast(a_cum, base + j, L)
                rowb = cumt_ref[base + j:base + j + 1, :]
                decay = jnp.where(causal, jnp.exp(colb - rowb), 0.0)
                ys.append(_dot((cb[g] * decay).astype(BF16), xpair))
            y_parts[g].append(jnp.where(first_half, ys[0], ys[1]))
    y_diag = [jnp.concatenate(y_parts[g], axis=1) for g in groups]

    prev = [state_ref[:, gcols[g]] for g in groups]
    y_off = [_dot(cm[g], prev[g].astype(BF16)) * jnp.exp(cum_x[g]) for g in groups]
    xw = [(xdt[g] * jnp.exp(last_x[g] - cum_x[g])).astype(BF16) for g in groups]
    for g in groups:
        state_ref[:, gcols[g]] = prev[g] * jnp.exp(last_x[g]) + _dot_tn(bm[g], xw[g])

    y = [y_diag[g] + y_off[g] + xs[g] * dsk_ref[:, gcols[g]] for g in groups]
    yg = [y[g] * _silu(z_ref[:, gcols[g]]) for g in groups]
    ms = [jnp.mean(yg[g] * yg[g], axis=-1, keepdims=True) for g in groups]
    for g in groups:
        o_ref[:, gcols[g]] = (yg[g] * lax.rsqrt(ms[g] + RMS_EPS)
                              * nw_ref[:, gcols[g]]).astype(o_ref.dtype)


def _ssd_mixer(proj, small, conv_w, conv_b, dtb_pad, alog_pad, dskip_x, norm_w, side):
    s = proj.shape[0]
    L, w, gn = SSD_CHUNK, SSD_WIDTH, SSD_GROUPS * SSD_STATE
    full = lambda c: (0, 0)
    n_steps = s // L
    s_in, s_out, s_shape, s_bytes = _side_cast_specs(side, n_steps, lambda c: c, blocks=n_steps)
    nbytes = (2 * L * (2 * w + 2 * gn + LANES) * 4 + 2 * L * w * 2
              + (L + CONV_PAD) * (w + 2 * gn) * 4 + LANES * L * 4 + SSD_STATE * w * 4
              + 8 * L * w * 4 + s_bytes)
    return pl.pallas_call(
        functools.partial(_ssd_kernel, has_side=True),
        grid=(n_steps,),
        in_specs=[
            pl.BlockSpec((L, w), lambda c: (c, OFF_X // w)),
            pl.BlockSpec((L, gn), lambda c: (c, OFF_B // gn)),
            pl.BlockSpec((L, gn), lambda c: (c, OFF_C // gn)),
            pl.BlockSpec((L, w), lambda c: (c, OFF_Z // w)),
            pl.BlockSpec((L, LANES), lambda c: (c, 0)),
            pl.BlockSpec((SSD_CONV, w), lambda c: (0, 0)),
            pl.BlockSpec((SSD_CONV, gn), lambda c: (0, w // gn)),
            pl.BlockSpec((SSD_CONV, gn), lambda c: (0, w // gn + 1)),
            pl.BlockSpec((1, w), lambda c: (0, 0)),
            pl.BlockSpec((1, gn), lambda c: (0, w // gn)),
            pl.BlockSpec((1, gn), lambda c: (0, w // gn + 1)),
            pl.BlockSpec((1, LANES), full),
            pl.BlockSpec((1, LANES), full),
            pl.BlockSpec((1, w), full),
            pl.BlockSpec((1, w), full),
            s_in,
        ],
        out_specs=[pl.BlockSpec((L, w), lambda c: (c, 0)), s_out],
        out_shape=[jax.ShapeDtypeStruct((s, w), BF16), s_shape],
        scratch_shapes=[
            pltpu.VMEM((L + CONV_PAD, w), F32),
            pltpu.VMEM((L + CONV_PAD, gn), F32),
            pltpu.VMEM((L + CONV_PAD, gn), F32),
            pltpu.VMEM((LANES, L), F32),
            pltpu.VMEM((SSD_STATE, w), F32),
        ],
        compiler_params=pltpu.CompilerParams(
            dimension_semantics=("arbitrary",),
            vmem_limit_bytes=_vmem_limit(nbytes)),
        name="ssd_mixer",
    )(proj, proj, proj, proj, small,
      conv_w, conv_w, conv_w, conv_b, conv_b, conv_b,
      dtb_pad, alog_pad, dskip_x, norm_w.reshape(1, -1), side)


def kernel(x, norm_mix_w, w_in, gla_gate_w2, gla_gate_b, gla_norm_w, conv_w, conv_b,
           dt_bias, a_log, d_skip, ssd_norm_w, w_out, norm_mlp_w, w_up, w_down, norm_f_w):
    bsz, seq, d = x.shape
    h = x.reshape(bsz * seq, d)
    depth = w_in.shape[0]
    gate_lo = N_GLA
    z_lo = gate_lo + GLA_GATE_RANK
    dt_lo = z_lo + SSD_WIDTH + SSD_XBC
    pad_heads = lambda v: jnp.pad(v, (SMALL_DT_OFF, 0)).reshape(1, LANES)

    for l in range(depth):
        w_in_t = w_in[l].T
        w_small_t = jnp.concatenate(
            [w_in_t[gate_lo:z_lo],
             jnp.zeros((SMALL_DT_OFF - GLA_GATE_RANK, d), w_in.dtype),
             w_in_t[dt_lo:dt_lo + SSD_HEADS]], axis=0)
        w2_pad = jnp.pad(gla_gate_w2[l], ((0, LANES - GLA_GATE_RANK), (0, 0))).astype(BF16)
        dskip_x = jnp.repeat(d_skip[l], SSD_HEAD_DIM).reshape(1, SSD_WIDTH)

        u, small = _rmsnorm(h, norm_mix_w[l], BF16, narrow_t=w_small_t)
        proj_gla, w_out_bf, w_ssd_bf = _proj(
            u, w_in_t, row0=0, n_out=N_GLA, tm=PROJ_TM, tn=PROJ_TN, out_dtype=F32,
            side=((w_out[l], 0, w_out.shape[1]), (w_in_t, z_lo, N_SSD)), name="in_proj_gla")
        proj_ssd = _proj(u, w_ssd_bf, row0=0, n_out=N_SSD, tm=1024, tn=1024,
                         out_dtype=F32, name="in_proj_ssd")

        o_gla = _gla_mixer(proj_gla, small, w2_pad, gla_gate_b[l], gla_norm_w[l])
        o_ssd, w_up_bf = _ssd_mixer(proj_ssd, small, conv_w[l], conv_b[l].reshape(1, -1),
                                    pad_heads(dt_bias[l]), pad_heads(a_log[l]), dskip_x,
                                    ssd_norm_w[l], w_up[l])

        h = _matmul_res_fullk([o_gla, o_ssd], w_out_bf, h, tm=512, tn=1024, name="out_proj")
        u = _rmsnorm(h, norm_mlp_w[l], BF16)
        hid, w_down_bf = _proj(u, w_up_bf, n_out=w_up.shape[2], tm=1024, tn=1024,
                               out_dtype=BF16, relu_sq=True,
                               side=((w_down[l], 0, w_down.shape[1]),), name="mlp_up")
        h = _matmul_res_fullk([hid], w_down_bf, h, tm=256, tn=1024, name="mlp_down")

    out = _rmsnorm(h, norm_f_w, x.dtype)
    return out.reshape(bsz, seq, d)
```
